```python
import jax, jax.numpy as jnp
from jax import lax
import numpy as np

D_MODEL = 2048
BATCH = 2
SEQ = 8192
DEPTH = 1

MLA_HEADS = 8
QK_NOPE_DIM = 128
QK_ROPE_DIM = 64
V_HEAD_DIM = 128
Q_LORA_RANK = 512
KV_LORA_RANK = 512
MLA_WIDTH = MLA_HEADS * V_HEAD_DIM
QK_HEAD_DIM = QK_NOPE_DIM + QK_ROPE_DIM
CONV_CHANNELS = D_MODEL - MLA_WIDTH
CONV_WIDTH = 31
CONV_PAD = CONV_WIDTH // 2
D_FF = 4 * D_MODEL
ROPE_BASE = 10000.0
Q_BLOCK = 128
LN_EPS = 1e-5
RMS_EPS = 1e-6
DEEPNORM_ALPHA = (2.0 * DEPTH) ** 0.25
DEEPNORM_BETA = (8.0 * DEPTH) ** -0.25
IN_COLS = Q_LORA_RANK + KV_LORA_RANK + QK_ROPE_DIM + 2 * CONV_CHANNELS

kernel_name = "hybrid_mla_conformer_deepnorm_encoder"


def layer_norm(x, g, b):
    xf = x.astype(jnp.float32)
    mu = jnp.mean(xf, axis=-1, keepdims=True)
    xc = xf - mu
    var = jnp.mean(jnp.square(xc), axis=-1, keepdims=True)
    y = xc * lax.rsqrt(var + LN_EPS)
    return (y * g.astype(jnp.float32) + b.astype(jnp.float32)).astype(x.dtype)


def rms_norm(x, g):
    xf = x.astype(jnp.float32)
    y = xf * lax.rsqrt(jnp.mean(jnp.square(xf), axis=-1, keepdims=True) + RMS_EPS)
    return (y * g.astype(jnp.float32)).astype(x.dtype)


def rope_tables(positions, dtype):
    half = QK_ROPE_DIM // 2
    inv_freq = ROPE_BASE ** (-jnp.arange(half, dtype=jnp.float32) * (2.0 / QK_ROPE_DIM))
    ang = positions.astype(jnp.float32)[..., None] * inv_freq
    return jnp.cos(ang).astype(dtype), jnp.sin(ang).astype(dtype)


def apply_rope(x, cos, sin):
    x1, x2 = jnp.split(x, 2, axis=-1)
    return jnp.concatenate([x1 * cos - x2 * sin, x2 * cos + x1 * sin], axis=-1)


def mla_attention(q_nope, q_rope, k_nope, k_rope, v):
    b, s, h, _ = q_nope.shape
    nb = s // Q_BLOCK
    scale = QK_HEAD_DIM ** -0.5
    qn = q_nope.reshape(b, nb, Q_BLOCK, h, QK_NOPE_DIM).transpose(1, 0, 2, 3, 4)
    qr = q_rope.reshape(b, nb, Q_BLOCK, h, QK_ROPE_DIM).transpose(1, 0, 2, 3, 4)

    def block(args):
        qn_b, qr_b = args
        scores = (jnp.einsum('bqhd,bkhd->bhqk', qn_b, k_nope)
                  + jnp.einsum('bqhr,bkr->bhqk', qr_b, k_rope))
        p = jax.nn.softmax(scores.astype(jnp.float32) * scale, axis=-1).astype(v.dtype)
        return jnp.einsum('bhqk,bkhd->bqhd', p, v)

    out = lax.map(block, (qn, qr))
    return out.transpose(1, 0, 2, 3, 4).reshape(b, s, h * V_HEAD_DIM)


def conformer_conv(u_in, conv_w, conv_b, g_ln, b_ln):
    a, gate = jnp.split(u_in, 2, axis=-1)
    u = a * jax.nn.sigmoid(gate)
    kern = conv_w.reshape(CONV_WIDTH, 1, CONV_CHANNELS).astype(u.dtype)
    u = lax.conv_general_dilated(
        u, kern, window_strides=(1,), padding=[(CONV_PAD, CONV_PAD)],
        dimension_numbers=('NWC', 'WIO', 'NWC'),
        feature_group_count=CONV_CHANNELS) + conv_b
    return jax.nn.silu(layer_norm(u, g_ln, b_ln))


def hybrid_layer(x, cos, sin, w_in, g_cq, w_uq, g_ckv, w_uk, w_uv, conv_w, conv_b,
                 g_conv_ln, b_conv_ln, w_out, g_ln1, b_ln1, w_ff1, w_ff2, g_ln2, b_ln2):
    b, s, _ = x.shape
    h = x @ w_in
    c_q, c_kv, k_rope, conv_in = jnp.split(
        h, [Q_LORA_RANK, Q_LORA_RANK + KV_LORA_RANK,
            Q_LORA_RANK + KV_LORA_RANK + QK_ROPE_DIM], axis=-1)
    q = (rms_norm(c_q, g_cq) @ w_uq).reshape(b, s, MLA_HEADS, QK_HEAD_DIM)
    q_nope, q_rope = jnp.split(q, [QK_NOPE_DIM], axis=-1)
    q_rope = apply_rope(q_rope, cos[:, :, None, :], sin[:, :, None, :])
    k_rope = apply_rope(k_rope, cos, sin)
    ckv = rms_norm(c_kv, g_ckv)
    k_nope = (ckv @ w_uk).reshape(b, s, MLA_HEADS, QK_NOPE_DIM)
    v = (ckv @ w_uv).reshape(b, s, MLA_HEADS, V_HEAD_DIM)
    attn_out = mla_attention(q_nope, q_rope, k_nope, k_rope, v)
    conv_out = conformer_conv(conv_in, conv_w, conv_b, g_conv_ln, b_conv_ln)
    mix = jnp.concatenate([attn_out, conv_out], axis=-1) @ w_out
    x = layer_norm(DEEPNORM_ALPHA * x + mix, g_ln1, b_ln1)
    ff = jnp.square(jax.nn.relu(x @ w_ff1)) @ w_ff2
    return layer_norm(DEEPNORM_ALPHA * x + ff, g_ln2, b_ln2)


def setup_inputs(seed: int = 0) -> dict:
    key = jax.random.key(seed)
    ks = jax.random.split(key, 24)
    f32 = jnp.float32

    def nrm(k, shape, scale):
        return jax.random.normal(k, shape, f32) * scale

    def gain(k, shape):
        return 1.0 + 0.02 * jax.random.normal(k, shape, f32)

    L = DEPTH
    beta = DEEPNORM_BETA
    return {
        "x": jax.random.normal(ks[0], (BATCH, SEQ, D_MODEL), f32),
        "positions": jnp.broadcast_to(jnp.arange(SEQ, dtype=jnp.int32), (BATCH, SEQ)),
        "ln_in_g": gain(ks[1], (D_MODEL,)),
        "ln_in_b": nrm(ks[2], (D_MODEL,), 0.02),
        "w_in": nrm(ks[3], (L, D_MODEL, IN_COLS), D_MODEL ** -0.5),
        "g_cq": gain(ks[4], (L, Q_LORA_RANK)),
        "w_uq": nrm(ks[5], (L, Q_LORA_RANK, MLA_HEADS * QK_HEAD_DIM), Q_LORA_RANK ** -0.5),
        "g_ckv": gain(ks[6], (L, KV_LORA_RANK)),
        "w_uk": nrm(ks[7], (L, KV_LORA_RANK, MLA_HEADS * QK_NOPE_DIM), KV_LORA_RANK ** -0.5),
        "w_uv": nrm(ks[8], (L, KV_LORA_RANK, MLA_HEADS * V_HEAD_DIM), beta * KV_LORA_RANK ** -0.5),
        "conv_w": nrm(ks[9], (L, CONV_WIDTH, CONV_CHANNELS), CONV_WIDTH ** -0.5),
        "conv_b": nrm(ks[10], (L, CONV_CHANNELS), 0.02),
        "g_conv_ln": gain(ks[11], (L, CONV_CHANNELS)),
        "b_conv_ln": nrm(ks[12], (L, CONV_CHANNELS), 0.02),
        "w_out": nrm(ks[13], (L, D_MODEL, D_MODEL), beta * D_MODEL ** -0.5),
        "g_ln1": gain(ks[14], (L, D_MODEL)),
        "b_ln1": nrm(ks[15], (L, D_MODEL), 0.02),
        "w_ff1": nrm(ks[16], (L, D_MODEL, D_FF), beta * D_MODEL ** -0.5),
        "w_ff2": nrm(ks[17], (L, D_FF, D_MODEL), beta * D_FF ** -0.5),
        "g_ln2": gain(ks[18], (L, D_MODEL)),
        "b_ln2": nrm(ks[19], (L, D_MODEL), 0.02),
    }


def reference(x, positions, ln_in_g, ln_in_b, w_in, g_cq, w_uq, g_ckv, w_uk, w_uv,
              conv_w, conv_b, g_conv_ln, b_conv_ln, w_out, g_ln1, b_ln1,
              w_ff1, w_ff2, g_ln2, b_ln2):
    cos, sin = rope_tables(positions, x.dtype)
    x = layer_norm(x, ln_in_g, ln_in_b)
    for l in range(DEPTH):
        x = hybrid_layer(x, cos, sin, w_in[l], g_cq[l], w_uq[l], g_ckv[l], w_uk[l], w_uv[l],
                         conv_w[l], conv_b[l], g_conv_ln[l], b_conv_ln[l], w_out[l],
                         g_ln1[l], b_ln1[l], w_ff1[l], w_ff2[l], g_ln2[l], b_ln2[l])
    return x
```

```python
import functools
import math

import jax
import jax.numpy as jnp
from jax import lax
from jax.experimental import pallas as pl
from jax.experimental.pallas import tpu as pltpu

F32 = jnp.float32
BF16 = jnp.bfloat16

D_MODEL = 2048
HEADS = 8
NOPE = 128
ROPE = 64
HALF = ROPE // 2
QK = NOPE + ROPE
QK_PAD = 256
V_DIM = 128
Q_RANK = 512
KV_RANK = 512
MLA_WIDTH = HEADS * V_DIM
CONV_CH = D_MODEL - MLA_WIDTH
CONV_W = 31
CONV_PAD = CONV_W // 2
HALO = 16
D_FF = 4 * D_MODEL
ROPE_BASE = 10000.0
LN_EPS = 1e-5
RMS_EPS = 1e-6
ALPHA = 2.0 ** 0.25
Q_SCALE = (QK ** -0.5) * math.log2(math.e)

VMEM_LIMIT = 56 * 1024 * 1024

NT_DIMS = (((1,), (1,)), ((), ()))


def _const_spec(shape):
    nd = len(shape)
    return pl.BlockSpec(shape, lambda *_: (0,) * nd, pipeline_mode=pl.Buffered(1))


def _layer_norm(x, g, b):
    mu = jnp.mean(x, axis=-1, keepdims=True)
    xc = x - mu
    var = jnp.mean(xc * xc, axis=-1, keepdims=True)
    return xc * lax.rsqrt(var + LN_EPS) * g + b


def _rope_kernel(pos_ref, invf_ref, cos_ref, sin_ref):
    ang = invf_ref[...] * pos_ref[0].astype(F32)
    cos_ref[0] = jnp.cos(ang)
    sin_ref[0] = jnp.sin(ang)


def _rope_tables(positions, ts=2048):
    b, s = positions.shape
    inv_freq = ROPE_BASE ** (-jnp.arange(HALF, dtype=F32) * (2.0 / ROPE))
    out = jax.ShapeDtypeStruct((b, HALF, s), F32)
    return pl.pallas_call(
        _rope_kernel,
        grid=(b, s // ts),
        in_specs=[pl.BlockSpec((1, 1, ts), lambda i, j: (i, 0, j)),
                  pl.BlockSpec((HALF, 1), lambda i, j: (0, 0))],
        out_specs=[pl.BlockSpec((1, HALF, ts), lambda i, j: (i, 0, j))] * 2,
        out_shape=[out, out],
        name="rope_tables",
    )(positions.reshape(b, 1, s), inv_freq.reshape(HALF, 1))


def _inproj_kernel(x_ref, lng_ref, lnb_ref, wc_ref, wkrT_ref, wconv_ref, gcq_ref, gckv_ref,
                   wuqT_ref, wuk_ref, wuvT_ref, cos_ref, sin_ref,
                   qT_ref, k_ref, vT_ref, u_ref):
    tm = x_ref.shape[1]
    xb = _layer_norm(x_ref[0], lng_ref[...], lnb_ref[...]).astype(BF16)
    cos = cos_ref[0]
    sin = sin_ref[0]

    c = jnp.dot(xb, wc_ref[...], preferred_element_type=F32)
    c_q = c[:, :Q_RANK]
    c_kv = c[:, Q_RANK:]
    rq = lax.rsqrt(jnp.mean(c_q * c_q, axis=-1, keepdims=True) + RMS_EPS) * Q_SCALE
    cqn = (c_q * rq * gcq_ref[...]).astype(BF16)
    rkv = lax.rsqrt(jnp.mean(c_kv * c_kv, axis=-1, keepdims=True) + RMS_EPS)
    ckvn = (c_kv * rkv * gckv_ref[...]).astype(BF16)

    qT = lax.dot_general(wuqT_ref[...], cqn, NT_DIMS, preferred_element_type=F32)
    zeros_q = jnp.zeros((QK_PAD - QK, tm), BF16)
    for h in range(HEADS):
        base = h * QK
        r1 = qT[base + NOPE:base + NOPE + HALF]
        r2 = qT[base + NOPE + HALF:base + QK]
        qT_ref[0, h, 0:NOPE, :] = qT[base:base + NOPE].astype(BF16)
        qT_ref[0, h, NOPE:NOPE + HALF, :] = (r1 * cos - r2 * sin).astype(BF16)
        qT_ref[0, h, NOPE + HALF:QK, :] = (r2 * cos + r1 * sin).astype(BF16)
        qT_ref[0, h, QK:QK_PAD, :] = zeros_q

    krT = lax.dot_general(wkrT_ref[...], xb, NT_DIMS, preferred_element_type=F32)
    k1 = krT[:HALF]
    k2 = krT[HALF:]
    kr_fullT = jnp.concatenate(
        [k1 * cos - k2 * sin, k2 * cos + k1 * sin, jnp.zeros((QK_PAD - QK, tm), F32)], axis=0)
    kr = kr_fullT.T.astype(BF16)

    kn = jnp.dot(ckvn, wuk_ref[...], preferred_element_type=F32)
    vT = lax.dot_general(wuvT_ref[...], ckvn, NT_DIMS, preferred_element_type=F32)
    for h in range(HEADS):
        k_ref[0, h, :, 0:NOPE] = kn[:, h * NOPE:(h + 1) * NOPE].astype(BF16)
        k_ref[0, h, :, NOPE:QK_PAD] = kr
        vT_ref[0, h] = vT[h * V_DIM:(h + 1) * V_DIM].astype(BF16)

    ag = jnp.dot(xb, wconv_ref[...], preferred_element_type=F32)
    u_ref[0] = ag[:, :CONV_CH] * jax.nn.sigmoid(ag[:, CONV_CH:])


def _in_proj(x, ln_g, ln_b, wc, wkrT, wconv, g_cq, g_ckv, wuqT, wuk, wuvT, cosT, sinT, tm=256):
    b, s, d = x.shape
    grid = (b, s // tm)
    out_shape = [
        jax.ShapeDtypeStruct((b, HEADS, QK_PAD, s), BF16),
        jax.ShapeDtypeStruct((b, HEADS, s, QK_PAD), BF16),
        jax.ShapeDtypeStruct((b, HEADS, V_DIM, s), BF16),
        jax.ShapeDtypeStruct((b, s, CONV_CH), F32),
    ]
    in_specs = [
        pl.BlockSpec((1, tm, d), lambda i, j: (i, j, 0)),
        _const_spec((1, d)), _const_spec((1, d)),
        _const_spec(wc.shape), _const_spec(wkrT.shape), _const_spec(wconv.shape),
        _const_spec((1, Q_RANK)), _const_spec((1, KV_RANK)),
        _const_spec(wuqT.shape), _const_spec(wuk.shape), _const_spec(wuvT.shape),
        pl.BlockSpec((1, HALF, tm), lambda i, j: (i, 0, j)),
        pl.BlockSpec((1, HALF, tm), lambda i, j: (i, 0, j)),
    ]
    out_specs = [
        pl.BlockSpec((1, HEADS, QK_PAD, tm), lambda i, j: (i, 0, 0, j)),
        pl.BlockSpec((1, HEADS, tm, QK_PAD), lambda i, j: (i, 0, j, 0)),
        pl.BlockSpec((1, HEADS, V_DIM, tm), lambda i, j: (i, 0, 0, j)),
        pl.BlockSpec((1, tm, CONV_CH), lambda i, j: (i, j, 0)),
    ]
    return pl.pallas_call(
        _inproj_kernel, grid=grid, in_specs=in_specs, out_specs=out_specs, out_shape=out_shape,
        compiler_params=pltpu.CompilerParams(
            dimension_semantics=("parallel", "parallel"), vmem_limit_bytes=VMEM_LIMIT),
        name="in_proj",
    )(x, ln_g.reshape(1, d), ln_b.reshape(1, d), wc, wkrT, wconv,
      g_cq.reshape(1, Q_RANK), g_ckv.reshape(1, KV_RANK), wuqT, wuk, wuvT, cosT, sinT)


CONV_RB = 32
CONV_CB = 512


def _conv_kernel(u_ref, prev_ref, next_ref, w_ref, b_ref, g_ref, beta_ref, o_ref, ext_ref, y_ref):
    tc = u_ref.shape[1]
    i = pl.program_id(1)
    last = pl.num_programs(1) - 1
    ext_ref[0:HALO, :] = jnp.where(i > 0, prev_ref[0], 0.0)
    ext_ref[HALO:HALO + tc, :] = u_ref[0]
    ext_ref[HALO + tc:HALO + tc + HALO, :] = jnp.where(i < last, next_ref[0], 0.0)
    off = HALO - CONV_PAD
    for r0 in range(0, tc, CONV_RB):
        for c0 in range(0, CONV_CH, CONV_CB):
            acc = jnp.zeros((CONV_RB, CONV_CB), F32)
            for k in range(CONV_W):
                acc = acc + w_ref[k:k + 1, c0:c0 + CONV_CB] * ext_ref[r0 + off + k:r0 + off + k + CONV_RB, c0:c0 + CONV_CB]
            y_ref[r0:r0 + CONV_RB, c0:c0 + CONV_CB] = acc + b_ref[:, c0:c0 + CONV_CB]
    y = _layer_norm(y_ref[...], g_ref[...], beta_ref[...])
    o_ref[0] = (y * jax.nn.sigmoid(y)).astype(o_ref.dtype)


def _conv_module(u, conv_w, conv_b, g, beta, tc=256):
    b, s, c = u.shape
    nh = tc // HALO
    n_halo_blocks = s // HALO
    return pl.pallas_call(
        _conv_kernel,
        grid=(b, s // tc),
        in_specs=[
            pl.BlockSpec((1, tc, c), lambda i, j: (i, j, 0)),
            pl.BlockSpec((1, HALO, c), lambda i, j: (i, jnp.maximum(j * nh - 1, 0), 0)),
            pl.BlockSpec((1, HALO, c), lambda i, j: (i, jnp.minimum((j + 1) * nh, n_halo_blocks - 1), 0)),
            _const_spec((CONV_W, c)), _const_spec((1, c)), _const_spec((1, c)), _const_spec((1, c)),
        ],
        out_specs=pl.BlockSpec((1, tc, c), lambda i, j: (i, j, 0)),
        out_shape=jax.ShapeDtypeStruct((b, s, c), BF16),
        scratch_shapes=[pltpu.VMEM((tc + 2 * HALO, c), F32), pltpu.VMEM((tc, c), F32)],
        compiler_params=pltpu.CompilerParams(dimension_semantics=("parallel", "parallel")),
        name="conv_module",
    )(u, u, u, conv_w, conv_b.reshape(1, c), g.reshape(1, c), beta.reshape(1, c))


def _attn_kernel(qT_ref, k_ref, vT_ref, o_ref, m_ref, l_ref, acc_ref, *, tk):
    s_len = k_ref.shape[2]
    qT = qT_ref[0, 0]
    m_ref[...] = jnp.full(m_ref.shape, -1e30, F32)
    l_ref[...] = jnp.zeros(l_ref.shape, F32)
    acc_ref[...] = jnp.zeros(acc_ref.shape, F32)

    def step(c, carry):
        start = pl.multiple_of(c * tk, tk)
        kc = k_ref[0, 0, pl.ds(start, tk), :]
        s = jnp.dot(kc, qT, preferred_element_type=F32)
        m_old = m_ref[...]
        m_new = jnp.maximum(m_old, jnp.max(s, axis=0, keepdims=True))
        alpha = jnp.exp2(m_old - m_new)
        p = jnp.exp2(s - m_new)
        l_ref[...] = alpha * l_ref[...] + jnp.sum(p, axis=0, keepdims=True)
        m_ref[...] = m_new
        vc = vT_ref[0, 0, :, pl.ds(start, tk)]
        acc_ref[...] = alpha * acc_ref[...] + jnp.dot(vc, p.astype(BF16), preferred_element_type=F32)
        return carry

    lax.fori_loop(0, s_len // tk, step, 0)
    o = acc_ref[...] / l_ref[...]
    o_ref[0] = o.T.astype(o_ref.dtype)


def _attention(qT, k, vT, tq=512, tk=512):
    b, h, _, s = qT.shape
    return pl.pallas_call(
        functools.partial(_attn_kernel, tk=tk),
        grid=(b, h, s // tq),
        in_specs=[
            pl.BlockSpec((1, 1, QK_PAD, tq), lambda bi, hi, qi: (bi, hi, 0, qi)),
            pl.BlockSpec((1, 1, s, QK_PAD), lambda bi, hi, qi: (bi, hi, 0, 0)),
            pl.BlockSpec((1, 1, V_DIM, s), lambda bi, hi, qi: (bi, hi, 0, 0)),
        ],
        out_specs=pl.BlockSpec((1, tq, V_DIM), lambda bi, hi, qi: (bi, qi, hi)),
        out_shape=jax.ShapeDtypeStruct((b, s, h * V_DIM), BF16),
        scratch_shapes=[pltpu.VMEM((1, tq), F32), pltpu.VMEM((1, tq), F32), pltpu.VMEM((V_DIM, tq), F32)],
        compiler_params=pltpu.CompilerParams(
            dimension_semantics=("parallel", "parallel", "arbitrary"), vmem_limit_bytes=VMEM_LIMIT),
        name="attention",
    )(qT, k, vT)


def _merge_kernel(x_ref, lng_ref, lnb_ref, a_ref, c_ref, wa_ref, wc_ref, g1_ref, b1_ref, o_ref):
    x0 = _layer_norm(x_ref[...], lng_ref[...], lnb_ref[...])
    mix = (jnp.dot(a_ref[...], wa_ref[...], preferred_element_type=F32)
           + jnp.dot(c_ref[...], wc_ref[...], preferred_element_type=F32))
    o_ref[...] = _layer_norm(ALPHA * x0 + mix, g1_ref[...], b1_ref[...])


def _merge(x2d, ln_g, ln_b, attn2d, conv2d, wa, wc, g1, b1, tm=512):
    t, d = x2d.shape
    row = lambda i: (i, 0)
    return pl.pallas_call(
        _merge_kernel,
        grid=(t // tm,),
        in_specs=[
            pl.BlockSpec((tm, d), row), _const_spec((1, d)), _const_spec((1, d)),
            pl.BlockSpec((tm, MLA_WIDTH), row), pl.BlockSpec((tm, CONV_CH), row),
            _const_spec(wa.shape), _const_spec(wc.shape), _const_spec((1, d)), _const_spec((1, d)),
        ],
        out_specs=pl.BlockSpec((tm, d), row),
        out_shape=jax.ShapeDtypeStruct((t, d), F32),
        compiler_params=pltpu.CompilerParams(
            dimension_semantics=("parallel",), vmem_limit_bytes=VMEM_LIMIT),
        name="merge",
    )(x2d, ln_g.reshape(1, d), ln_b.reshape(1, d), attn2d, conv2d, wa, wc, g1.reshape(1, d), b1.reshape(1, d))


def _ffn_kernel(x_ref, w1_ref, w2_ref, g_ref, b_ref, o_ref, xb_ref):
    j = pl.program_id(1)

    @pl.when(j == 0)
    def _():
        xb_ref[...] = x_ref[...].astype(BF16)

    h = jnp.dot(xb_ref[...], w1_ref[...], preferred_element_type=F32)
    h = jnp.maximum(h, 0.0)
    contrib = jnp.dot((h * h).astype(BF16), w2_ref[...], preferred_element_type=F32)

    @pl.when(j == 0)
    def _():
        o_ref[...] = contrib

    @pl.when(j > 0)
    def _():
        o_ref[...] += contrib

    @pl.when(j == pl.num_programs(1) - 1)
    def _():
        o_ref[...] = _layer_norm(ALPHA * x_ref[...] + o_ref[...], g_ref[...], b_ref[...])


def _ffn(x1, w1, w2, g2, b2, tm=512, tf=1024):
    t, d = x1.shape
    f = w1.shape[1]
    return pl.pallas_call(
        _ffn_kernel,
        grid=(t // tm, f // tf),
        in_specs=[
            pl.BlockSpec((tm, d), lambda i, j: (i, 0)),
            pl.BlockSpec((d, tf), lambda i, j: (0, j)),
            pl.BlockSpec((tf, d), lambda i, j: (j, 0)),
            _const_spec((1, d)), _const_spec((1, d)),
        ],
        out_specs=pl.BlockSpec((tm, d), lambda i, j: (i, 0)),
        out_shape=jax.ShapeDtypeStruct((t, d), F32),
        scratch_shapes=[pltpu.VMEM((tm, d), BF16)],
        compiler_params=pltpu.CompilerParams(
            dimension_semantics=("parallel", "arbitrary"), vmem_limit_bytes=VMEM_LIMIT),
        name="ffn",
    )(x1, w1, w2, g2.reshape(1, d), b2.reshape(1, d))


def kernel(x, positions, ln_in_g, ln_in_b, w_in, g_cq, w_uq, g_ckv, w_uk, w_uv, conv_w, conv_b,
           g_conv_ln, b_conv_ln, w_out, g_ln1, b_ln1, w_ff1, w_ff2, g_ln2, b_ln2):
    b, s, d = x.shape
    w_in0 = w_in[0]
    o_kr = Q_RANK + KV_RANK
    wc = w_in0[:, :o_kr].astype(BF16)
    wkrT = w_in0[:, o_kr:o_kr + ROPE].T.astype(BF16)
    wconv = w_in0[:, o_kr + ROPE:].astype(BF16)
    wuqT = w_uq[0].T.astype(BF16)
    wuk = w_uk[0].astype(BF16)
    wuvT = w_uv[0].T.astype(BF16)
    wa = w_out[0, :MLA_WIDTH].astype(BF16)
    wcv = w_out[0, MLA_WIDTH:].astype(BF16)
    w1 = w_ff1[0].astype(BF16)
    w2 = w_ff2[0].astype(BF16)

    cosT, sinT = _rope_tables(positions)
    qT, k, vT, u = _in_proj(x, ln_in_g, ln_in_b, wc, wkrT, wconv, g_cq[0], g_ckv[0],
                            wuqT, wuk, wuvT, cosT, sinT)
    conv_out = _conv_module(u, conv_w[0], conv_b[0], g_conv_ln[0], b_conv_ln[0])
    attn = _attention(qT, k, vT)
    x2d = x.reshape(b * s, d)
    x1 = _merge(x2d, ln_in_g, ln_in_b, attn.reshape(b * s, MLA_WIDTH), conv_out.reshape(b * s, CONV_CH),
                wa, wcv, g_ln1[0], b_ln1[0])
    out = _ffn(x1, w1, w2, g_ln2[0], b_ln2[0])
    return out.reshape(b, s, d)
```

```python
import functools
import math

import jax
import jax.numpy as jnp
from jax import lax
from jax.experimental import pallas as pl
from jax.experimental.pallas import tpu as pltpu

F32 = jnp.float32
BF16 = jnp.bfloat16

D_MODEL = 2048
HEADS = 8
NOPE = 128
ROPE = 64
HALF = ROPE // 2
QK = NOPE + ROPE
QK_PAD = 256
V_DIM = 128
VT_ROWS = V_DIM + 16
Q_RANK = 512
KV_RANK = 512
MLA_WIDTH = HEADS * V_DIM
CONV_CH = D_MODEL - MLA_WIDTH
CONV_W = 31
CONV_PAD = CONV_W // 2
HALO = 16
D_FF = 4 * D_MODEL
ROPE_BASE = 10000.0
LN_EPS = 1e-5
RMS_EPS = 1e-6
ALPHA = 2.0 ** 0.25
Q_SCALE = (QK ** -0.5) * math.log2(math.e)

VMEM_LIMIT = 56 * 1024 * 1024

NT_DIMS = (((1,), (1,)), ((), ()))


def _const_spec(shape):
    nd = len(shape)
    return pl.BlockSpec(shape, lambda *_: (0,) * nd, pipeline_mode=pl.Buffered(1))


def _layer_norm(x, g, b):
    mu = jnp.mean(x, axis=-1, keepdims=True)
    xc = x - mu
    var = jnp.mean(xc * xc, axis=-1, keepdims=True)
    return xc * lax.rsqrt(var + LN_EPS) * g + b


def _rope_kernel(pos_ref, invf_ref, cos_ref, sin_ref):
    ang = invf_ref[...] * pos_ref[0].astype(F32)
    cos_ref[0] = jnp.cos(ang)
    sin_ref[0] = jnp.sin(ang)


def _rope_tables(positions, ts=2048):
    b, s = positions.shape
    inv_freq = ROPE_BASE ** (-jnp.arange(HALF, dtype=F32) * (2.0 / ROPE))
    out = jax.ShapeDtypeStruct((b, HALF, s), F32)
    return pl.pallas_call(
        _rope_kernel,
        grid=(b, s // ts),
        in_specs=[pl.BlockSpec((1, 1, ts), lambda i, j: (i, 0, j)),
                  pl.BlockSpec((HALF, 1), lambda i, j: (0, 0))],
        out_specs=[pl.BlockSpec((1, HALF, ts), lambda i, j: (i, 0, j))] * 2,
        out_shape=[out, out],
        name="rope_tables",
    )(positions.reshape(b, 1, s), inv_freq.reshape(HALF, 1))


def _inproj_kernel(x_ref, lng_ref, lnb_ref, wc_ref, wkrT_ref, wconv_ref, gcq_ref, gckv_ref,
                   wuqT_ref, wuk_ref, wuvT_ref, cos_ref, sin_ref,
                   qT_ref, k_ref, vT_ref, u_ref):
    tm = x_ref.shape[1]
    xb = _layer_norm(x_ref[0], lng_ref[...], lnb_ref[...]).astype(BF16)
    cos = cos_ref[0]
    sin = sin_ref[0]

    c = jnp.dot(xb, wc_ref[...], preferred_element_type=F32)
    c_q = c[:, :Q_RANK]
    c_kv = c[:, Q_RANK:]
    rq = lax.rsqrt(jnp.mean(c_q * c_q, axis=-1, keepdims=True) + RMS_EPS) * Q_SCALE
    cqn = (c_q * rq * gcq_ref[...]).astype(BF16)
    rkv = lax.rsqrt(jnp.mean(c_kv * c_kv, axis=-1, keepdims=True) + RMS_EPS)
    ckvn = (c_kv * rkv * gckv_ref[...]).astype(BF16)

    qT = lax.dot_general(wuqT_ref[...], cqn, NT_DIMS, preferred_element_type=F32)
    zeros_q = jnp.zeros((QK_PAD - QK, tm), BF16)
    for h in range(HEADS):
        base = h * QK
        r1 = qT[base + NOPE:base + NOPE + HALF]
        r2 = qT[base + NOPE + HALF:base + QK]
        qT_ref[0, h, 0:NOPE, :] = qT[base:base + NOPE].astype(BF16)
        qT_ref[0, h, NOPE:NOPE + HALF, :] = (r1 * cos - r2 * sin).astype(BF16)
        qT_ref[0, h, NOPE + HALF:QK, :] = (r2 * cos + r1 * sin).astype(BF16)
        qT_ref[0, h, QK:QK_PAD, :] = zeros_q

    krT = lax.dot_general(wkrT_ref[...], xb, NT_DIMS, preferred_element_type=F32)
    k1 = krT[:HALF]
    k2 = krT[HALF:]
    kr_fullT = jnp.concatenate(
        [k1 * cos - k2 * sin, k2 * cos + k1 * sin, jnp.zeros((QK_PAD - QK, tm), F32)], axis=0)
    kr = kr_fullT.T.astype(BF16)

    kn = jnp.dot(ckvn, wuk_ref[...], preferred_element_type=F32)
    vT = lax.dot_general(wuvT_ref[...], ckvn, NT_DIMS, preferred_element_type=F32)
    row = lax.broadcasted_iota(jnp.int32, (VT_ROWS - V_DIM, tm), 0)
    ones_row = jnp.where(row == 0, 1.0, 0.0).astype(BF16)
    for h in range(HEADS):
        k_ref[0, h, :, 0:NOPE] = kn[:, h * NOPE:(h + 1) * NOPE].astype(BF16)
        k_ref[0, h, :, NOPE:QK_PAD] = kr
        vT_ref[0, h, 0:V_DIM, :] = vT[h * V_DIM:(h + 1) * V_DIM].astype(BF16)
        vT_ref[0, h, V_DIM:VT_ROWS, :] = ones_row

    ag = jnp.dot(xb, wconv_ref[...], preferred_element_type=F32)
    u_ref[0] = ag[:, :CONV_CH] * jax.nn.sigmoid(ag[:, CONV_CH:])


def _in_proj(x, ln_g, ln_b, wc, wkrT, wconv, g_cq, g_ckv, wuqT, wuk, wuvT, cosT, sinT, tm=256):
    b, s, d = x.shape
    grid = (b, s // tm)
    out_shape = [
        jax.ShapeDtypeStruct((b, HEADS, QK_PAD, s), BF16),
        jax.ShapeDtypeStruct((b, HEADS, s, QK_PAD), BF16),
        jax.ShapeDtypeStruct((b, HEADS, VT_ROWS, s), BF16),
        jax.ShapeDtypeStruct((b, s, CONV_CH), F32),
    ]
    in_specs = [
        pl.BlockSpec((1, tm, d), lambda i, j: (i, j, 0)),
        _const_spec((1, d)), _const_spec((1, d)),
        _const_spec(wc.shape), _const_spec(wkrT.shape), _const_spec(wconv.shape),
        _const_spec((1, Q_RANK)), _const_spec((1, KV_RANK)),
        _const_spec(wuqT.shape), _const_spec(wuk.shape), _const_spec(wuvT.shape),
        pl.BlockSpec((1, HALF, tm), lambda i, j: (i, 0, j)),
        pl.BlockSpec((1, HALF, tm), lambda i, j: (i, 0, j)),
    ]
    out_specs = [
        pl.BlockSpec((1, HEADS, QK_PAD, tm), lambda i, j: (i, 0, 0, j)),
        pl.BlockSpec((1, HEADS, tm, QK_PAD), lambda i, j: (i, 0, j, 0)),
        pl.BlockSpec((1, HEADS, VT_ROWS, tm), lambda i, j: (i, 0, 0, j)),
        pl.BlockSpec((1, tm, CONV_CH), lambda i, j: (i, j, 0)),
    ]
    return pl.pallas_call(
        _inproj_kernel, grid=grid, in_specs=in_specs, out_specs=out_specs, out_shape=out_shape,
        compiler_params=pltpu.CompilerParams(
            dimension_semantics=("parallel", "parallel"), vmem_limit_bytes=VMEM_LIMIT),
        name="in_proj",
    )(x, ln_g.reshape(1, d), ln_b.reshape(1, d), wc, wkrT, wconv,
      g_cq.reshape(1, Q_RANK), g_ckv.reshape(1, KV_RANK), wuqT, wuk, wuvT, cosT, sinT)


CONV_RB = 32
CONV_CB = 512


def _conv_kernel(u_ref, prev_ref, next_ref, w_ref, b_ref, g_ref, beta_ref, o_ref, ext_ref, y_ref):
    tc = u_ref.shape[1]
    i = pl.program_id(1)
    last = pl.num_programs(1) - 1
    ext_ref[0:HALO, :] = jnp.where(i > 0, prev_ref[0], 0.0)
    ext_ref[HALO:HALO + tc, :] = u_ref[0]
    ext_ref[HALO + tc:HALO + tc + HALO, :] = jnp.where(i < last, next_ref[0], 0.0)
    off = HALO - CONV_PAD
    for r0 in range(0, tc, CONV_RB):
        for c0 in range(0, CONV_CH, CONV_CB):
            acc = jnp.zeros((CONV_RB, CONV_CB), F32)
            for k in range(CONV_W):
                acc = acc + w_ref[k:k + 1, c0:c0 + CONV_CB] * ext_ref[r0 + off + k:r0 + off + k + CONV_RB, c0:c0 + CONV_CB]
            y_ref[r0:r0 + CONV_RB, c0:c0 + CONV_CB] = acc + b_ref[:, c0:c0 + CONV_CB]
    y = _layer_norm(y_ref[...], g_ref[...], beta_ref[...])
    o_ref[0] = (y * jax.nn.sigmoid(y)).astype(o_ref.dtype)


def _conv_module(u, conv_w, conv_b, g, beta, tc=256):
    b, s, c = u.shape
    nh = tc // HALO
    n_halo_blocks = s // HALO
    return pl.pallas_call(
        _conv_kernel,
        grid=(b, s // tc),
        in_specs=[
            pl.BlockSpec((1, tc, c), lambda i, j: (i, j, 0)),
            pl.BlockSpec((1, HALO, c), lambda i, j: (i, jnp.maximum(j * nh - 1, 0), 0)),
            pl.BlockSpec((1, HALO, c), lambda i, j: (i, jnp.minimum((j + 1) * nh, n_halo_blocks - 1), 0)),
            _const_spec((CONV_W, c)), _const_spec((1, c)), _const_spec((1, c)), _const_spec((1, c)),
        ],
        out_specs=pl.BlockSpec((1, tc, c), lambda i, j: (i, j, 0)),
        out_shape=jax.ShapeDtypeStruct((b, s, c), BF16),
        scratch_shapes=[pltpu.VMEM((tc + 2 * HALO, c), F32), pltpu.VMEM((tc, c), F32)],
        compiler_params=pltpu.CompilerParams(dimension_semantics=("parallel", "parallel")),
        name="conv_module",
    )(u, u, u, conv_w, conv_b.reshape(1, c), g.reshape(1, c), beta.reshape(1, c))


def _attn_kernel(qT_ref, k_ref, vT_ref, o_ref, s_ref, acc_ref, *, tk1, tk2):
    s_len = k_ref.shape[2]
    tq = qT_ref.shape[3]
    qT = qT_ref[0, 0]

    def scores(c, mpart):
        start = pl.multiple_of(c * tk1, tk1)
        s = jnp.dot(k_ref[0, 0, pl.ds(start, tk1), :], qT, preferred_element_type=F32)
        s_ref[pl.ds(start, tk1), :] = s
        return jnp.maximum(mpart, jnp.max(s.reshape(tk1 // 8, 8, tq), axis=0))

    mpart = lax.fori_loop(0, s_len // tk1, scores, jnp.full((8, tq), -jnp.inf, F32))
    m = jnp.max(mpart, axis=0, keepdims=True)

    acc_ref[...] = jnp.zeros(acc_ref.shape, F32)

    def weighted(c, carry):
        start = pl.multiple_of(c * tk2, tk2)
        p = jnp.exp2(s_ref[pl.ds(start, tk2), :] - m).astype(BF16)
        acc_ref[...] += jnp.dot(vT_ref[0, 0, :, pl.ds(start, tk2)], p, preferred_element_type=F32)
        return carry

    lax.fori_loop(0, s_len // tk2, weighted, 0)
    o = acc_ref[0:V_DIM, :] / acc_ref[V_DIM:V_DIM + 1, :]
    o_ref[0] = o.T.astype(o_ref.dtype)


def _attention(qT, k, vT, tq=512, tk1=4096, tk2=4096):
    b, h, _, s = qT.shape
    return pl.pallas_call(
        functools.partial(_attn_kernel, tk1=tk1, tk2=tk2),
        grid=(b, h, s // tq),
        in_specs=[
            pl.BlockSpec((1, 1, QK_PAD, tq), lambda bi, hi, qi: (bi, hi, 0, qi)),
            pl.BlockSpec((1, 1, s, QK_PAD), lambda bi, hi, qi: (bi, hi, 0, 0)),
            pl.BlockSpec((1, 1, VT_ROWS, s), lambda bi, hi, qi: (bi, hi, 0, 0)),
        ],
        out_specs=pl.BlockSpec((1, tq, V_DIM), lambda bi, hi, qi: (bi, qi, hi)),
        out_shape=jax.ShapeDtypeStruct((b, s, h * V_DIM), BF16),
        scratch_shapes=[pltpu.VMEM((s, tq), F32), pltpu.VMEM((VT_ROWS, tq), F32)],
        compiler_params=pltpu.CompilerParams(
            dimension_semantics=("parallel", "parallel", "arbitrary"), vmem_limit_bytes=VMEM_LIMIT),
        name="attention",
    )(qT, k, vT)


def _merge_kernel(x_ref, lng_ref, lnb_ref, a_ref, c_ref, wa_ref, wc_ref, g1_ref, b1_ref, o_ref):
    x0 = _layer_norm(x_ref[...], lng_ref[...], lnb_ref[...])
    mix = (jnp.dot(a_ref[...], wa_ref[...], preferred_element_type=F32)
           + jnp.dot(c_ref[...], wc_ref[...], preferred_element_type=F32))
    o_ref[...] = _layer_norm(ALPHA * x0 + mix, g1_ref[...], b1_ref[...])


def _merge(x2d, ln_g, ln_b, attn2d, conv2d, wa, wc, g1, b1, tm=512):
    t, d = x2d.shape
    row = lambda i: (i, 0)
    return pl.pallas_call(
        _merge_kernel,
        grid=(t // tm,),
        in_specs=[
            pl.BlockSpec((tm, d), row), _const_spec((1, d)), _const_spec((1, d)),
            pl.BlockSpec((tm, MLA_WIDTH), row), pl.BlockSpec((tm, CONV_CH), row),
            _const_spec(wa.shape), _const_spec(wc.shape), _const_spec((1, d)), _const_spec((1, d)),
        ],
        out_specs=pl.BlockSpec((tm, d), row),
        out_shape=jax.ShapeDtypeStruct((t, d), F32),
        compiler_params=pltpu.CompilerParams(
            dimension_semantics=("parallel",), vmem_limit_bytes=VMEM_LIMIT),
        name="merge",
    )(x2d, ln_g.reshape(1, d), ln_b.reshape(1, d), attn2d, conv2d, wa, wc, g1.reshape(1, d), b1.reshape(1, d))


def _ffn_kernel(x_ref, w1_ref, w2_ref, g_ref, b_ref, o_ref, xb_ref):
    j = pl.program_id(1)

    @pl.when(j == 0)
    def _():
        xb_ref[...] = x_ref[...].astype(BF16)

    h = jnp.dot(xb_ref[...], w1_ref[...], preferred_element_type=F32)
    h = jnp.maximum(h, 0.0)
    contrib = jnp.dot((h * h).astype(BF16), w2_ref[...], preferred_element_type=F32)

    @pl.when(j == 0)
    def _():
        o_ref[...] = contrib

    @pl.when(j > 0)
    def _():
        o_ref[...] += contrib

    @pl.when(j == pl.num_programs(1) - 1)
    def _():
        o_ref[...] = _layer_norm(ALPHA * x_ref[...] + o_ref[...], g_ref[...], b_ref[...])


def _ffn(x1, w1, w2, g2, b2, tm=512, tf=1024):
    t, d = x1.shape
    f = w1.shape[1]
    return pl.pallas_call(
        _ffn_kernel,
        grid=(t // tm, f // tf),
        in_specs=[
            pl.BlockSpec((tm, d), lambda i, j: (i, 0)),
            pl.BlockSpec((d, tf), lambda i, j: (0, j)),
            pl.BlockSpec((tf, d), lambda i, j: (j, 0)),
            _const_spec((1, d)), _const_spec((1, d)),
        ],
        out_specs=pl.BlockSpec((tm, d), lambda i, j: (i, 0)),
        out_shape=jax.ShapeDtypeStruct((t, d), F32),
        scratch_shapes=[pltpu.VMEM((tm, d), BF16)],
        compiler_params=pltpu.CompilerParams(
            dimension_semantics=("parallel", "arbitrary"), vmem_limit_bytes=VMEM_LIMIT),
        name="ffn",
    )(x1, w1, w2, g2.reshape(1, d), b2.reshape(1, d))


def kernel(x, positions, ln_in_g, ln_in_b, w_in, g_cq, w_uq, g_ckv, w_uk, w_uv, conv_w, conv_b,
           g_conv_ln, b_conv_ln, w_out, g_ln1, b_ln1, w_ff1, w_ff2, g_ln2, b_ln2):
    b, s, d = x.shape
    w_in0 = w_in[0]
    o_kr = Q_RANK + KV_RANK
    wc = w_in0[:, :o_kr].astype(BF16)
    wkrT = w_in0[:, o_kr:o_kr + ROPE].T.astype(BF16)
    wconv = w_in0[:, o_kr + ROPE:].astype(BF16)
    wuqT = w_uq[0].T.astype(BF16)
    wuk = w_uk[0].astype(BF16)
    wuvT = w_uv[0].T.astype(BF16)
    wa = w_out[0, :MLA_WIDTH].astype(BF16)
    wcv = w_out[0, MLA_WIDTH:].astype(BF16)
    w1 = w_ff1[0].astype(BF16)
    w2 = w_ff2[0].astype(BF16)

    cosT, sinT = _rope_tables(positions)
    qT, k, vT, u = _in_proj(x, ln_in_g, ln_in_b, wc, wkrT, wconv, g_cq[0], g_ckv[0],
                            wuqT, wuk, wuvT, cosT, sinT)
    conv_out = _conv_module(u, conv_w[0], conv_b[0], g_conv_ln[0], b_conv_ln[0])
    attn = _attention(qT, k, vT)
    x2d = x.reshape(b * s, d)
    x1 = _merge(x2d, ln_in_g, ln_in_b, attn.reshape(b * s, MLA_WIDTH), conv_out.reshape(b * s, CONV_CH),
                wa, wcv, g_ln1[0], b_ln1[0])
    out = _ffn(x1, w1, w2, g_ln2[0], b_ln2[0])
    return out.reshape(b, s, d)
```

```python
import functools
import math

import jax
import jax.numpy as jnp
from jax import lax
from jax.experimental import pallas as pl
from jax.experimental.pallas import tpu as pltpu

F32 = jnp.float32
BF16 = jnp.bfloat16

D_MODEL = 2048
HEADS = 8
NOPE = 128
ROPE = 64
HALF = ROPE // 2
QK = NOPE + ROPE
QK_PAD = 256
V_DIM = 128
VT_ROWS = V_DIM + 16
Q_RANK = 512
KV_RANK = 512
MLA_WIDTH = HEADS * V_DIM
CONV_CH = D_MODEL - MLA_WIDTH
CONV_W = 31
CONV_PAD = CONV_W // 2
HALO = 16
D_FF = 4 * D_MODEL
ROPE_BASE = 10000.0
LN_EPS = 1e-5
RMS_EPS = 1e-6
ALPHA = 2.0 ** 0.25
Q_SCALE = (QK ** -0.5) * math.log2(math.e)

VMEM_LIMIT = 56 * 1024 * 1024

NT_DIMS = (((1,), (1,)), ((), ()))


def _const_spec(shape):
    nd = len(shape)
    return pl.BlockSpec(shape, lambda *_: (0,) * nd, pipeline_mode=pl.Buffered(1))


def _layer_norm(x, g, b):
    mu = jnp.mean(x, axis=-1, keepdims=True)
    xc = x - mu
    var = jnp.mean(xc * xc, axis=-1, keepdims=True)
    return xc * lax.rsqrt(var + LN_EPS) * g + b


def _rope_kernel(pos_ref, invf_ref, cos_ref, sin_ref):
    ang = invf_ref[...] * pos_ref[0].astype(F32)
    cos_ref[0] = jnp.cos(ang)
    sin_ref[0] = jnp.sin(ang)


def _rope_tables(positions, ts=2048):
    b, s = positions.shape
    inv_freq = ROPE_BASE ** (-jnp.arange(HALF, dtype=F32) * (2.0 / ROPE))
    out = jax.ShapeDtypeStruct((b, HALF, s), F32)
    return pl.pallas_call(
        _rope_kernel,
        grid=(b, s // ts),
        in_specs=[pl.BlockSpec((1, 1, ts), lambda i, j: (i, 0, j)),
                  pl.BlockSpec((HALF, 1), lambda i, j: (0, 0))],
        out_specs=[pl.BlockSpec((1, HALF, ts), lambda i, j: (i, 0, j))] * 2,
        out_shape=[out, out],
        name="rope_tables",
    )(positions.reshape(b, 1, s), inv_freq.reshape(HALF, 1))


def _inproj_kernel(x_ref, lng_ref, lnb_ref, wc_ref, wkrT_ref, wconv_ref, gcq_ref, gckv_ref,
                   wuqT_ref, wuk_ref, wuvT_ref, cos_ref, sin_ref,
                   qT_ref, k_ref, vT_ref, u_ref):
    tm = x_ref.shape[1]
    xb = _layer_norm(x_ref[0], lng_ref[...], lnb_ref[...]).astype(BF16)
    cos = cos_ref[0]
    sin = sin_ref[0]

    c = jnp.dot(xb, wc_ref[...], preferred_element_type=F32)
    c_q = c[:, :Q_RANK]
    c_kv = c[:, Q_RANK:]
    rq = lax.rsqrt(jnp.mean(c_q * c_q, axis=-1, keepdims=True) + RMS_EPS) * Q_SCALE
    cqn = (c_q * rq * gcq_ref[...]).astype(BF16)
    rkv = lax.rsqrt(jnp.mean(c_kv * c_kv, axis=-1, keepdims=True) + RMS_EPS)
    ckvn = (c_kv * rkv * gckv_ref[...]).astype(BF16)

    qT = lax.dot_general(wuqT_ref[...], cqn, NT_DIMS, preferred_element_type=F32)
    zeros_q = jnp.zeros((QK_PAD - QK, tm), BF16)
    for h in range(HEADS):
        base = h * QK
        r1 = qT[base + NOPE:base + NOPE + HALF]
        r2 = qT[base + NOPE + HALF:base + QK]
        qT_ref[0, h, 0:NOPE, :] = qT[base:base + NOPE].astype(BF16)
        qT_ref[0, h, NOPE:NOPE + HALF, :] = (r1 * cos - r2 * sin).astype(BF16)
        qT_ref[0, h, NOPE + HALF:QK, :] = (r2 * cos + r1 * sin).astype(BF16)
        qT_ref[0, h, QK:QK_PAD, :] = zeros_q

    krT = lax.dot_general(wkrT_ref[...], xb, NT_DIMS, preferred_element_type=F32)
    k1 = krT[:HALF]
    k2 = krT[HALF:]
    kr_fullT = jnp.concatenate(
        [k1 * cos - k2 * sin, k2 * cos + k1 * sin, jnp.zeros((QK_PAD - QK, tm), F32)], axis=0)
    kr = kr_fullT.T.astype(BF16)

    kn = jnp.dot(ckvn, wuk_ref[...], preferred_element_type=F32)
    vT = lax.dot_general(wuvT_ref[...], ckvn, NT_DIMS, preferred_element_type=F32)
    row = lax.broadcasted_iota(jnp.int32, (VT_ROWS - V_DIM, tm), 0)
    ones_row = jnp.where(row == 0, 1.0, 0.0).astype(BF16)
    for h in range(HEADS):
        k_ref[0, h, :, 0:NOPE] = kn[:, h * NOPE:(h + 1) * NOPE].astype(BF16)
        k_ref[0, h, :, NOPE:QK_PAD] = kr
        vT_ref[0, h, 0:V_DIM, :] = vT[h * V_DIM:(h + 1) * V_DIM].astype(BF16)
        vT_ref[0, h, V_DIM:VT_ROWS, :] = ones_row

    ag = jnp.dot(xb, wconv_ref[...], preferred_element_type=F32)
    u = ag[:, :CONV_CH] * jax.nn.sigmoid(ag[:, CONV_CH:])
    u_ref[0] = u.reshape(tm, SUB, LANE)


def _in_proj(x, ln_g, ln_b, wc, wkrT, wconv, g_cq, g_ckv, wuqT, wuk, wuvT, cosT, sinT, tm=256):
    b, s, d = x.shape
    grid = (b, s // tm)
    out_shape = [
        jax.ShapeDtypeStruct((b, HEADS, QK_PAD, s), BF16),
        jax.ShapeDtypeStruct((b, HEADS, s, QK_PAD), BF16),
        jax.ShapeDtypeStruct((b, HEADS, VT_ROWS, s), BF16),
        jax.ShapeDtypeStruct((b, s, SUB, LANE), F32),
    ]
    in_specs = [
        pl.BlockSpec((1, tm, d), lambda i, j: (i, j, 0)),
        _const_spec((1, d)), _const_spec((1, d)),
        _const_spec(wc.shape), _const_spec(wkrT.shape), _const_spec(wconv.shape),
        _const_spec((1, Q_RANK)), _const_spec((1, KV_RANK)),
        _const_spec(wuqT.shape), _const_spec(wuk.shape), _const_spec(wuvT.shape),
        pl.BlockSpec((1, HALF, tm), lambda i, j: (i, 0, j)),
        pl.BlockSpec((1, HALF, tm), lambda i, j: (i, 0, j)),
    ]
    out_specs = [
        pl.BlockSpec((1, HEADS, QK_PAD, tm), lambda i, j: (i, 0, 0, j)),
        pl.BlockSpec((1, HEADS, tm, QK_PAD), lambda i, j: (i, 0, j, 0)),
        pl.BlockSpec((1, HEADS, VT_ROWS, tm), lambda i, j: (i, 0, 0, j)),
        pl.BlockSpec((1, tm, SUB, LANE), lambda i, j: (i, j, 0, 0)),
    ]
    return pl.pallas_call(
        _inproj_kernel, grid=grid, in_specs=in_specs, out_specs=out_specs, out_shape=out_shape,
        compiler_params=pltpu.CompilerParams(
            dimension_semantics=("parallel", "parallel"), vmem_limit_bytes=VMEM_LIMIT),
        name="in_proj",
    )(x, ln_g.reshape(1, d), ln_b.reshape(1, d), wc, wkrT, wconv,
      g_cq.reshape(1, Q_RANK), g_ckv.reshape(1, KV_RANK), wuqT, wuk, wuvT, cosT, sinT)


CONV_TB = 16
SUB = 8
LANE = 128


def _conv_kernel(u_ref, prev_ref, next_ref, w_ref, b_ref, y_ref, ext_ref):
    tc = u_ref.shape[1]
    i = pl.program_id(1)
    last = pl.num_programs(1) - 1
    ext_ref[0:HALO] = jnp.where(i > 0, prev_ref[0], 0.0)
    ext_ref[HALO:HALO + tc] = u_ref[0]
    ext_ref[HALO + tc:HALO + tc + HALO] = jnp.where(i < last, next_ref[0], 0.0)
    off = HALO - CONV_PAD
    bias = b_ref[...]

    def group(gi, carry):
        t0 = pl.multiple_of(gi * CONV_TB, CONV_TB)
        acc = jnp.zeros((CONV_TB, SUB, LANE), F32)
        for k in range(CONV_W):
            acc = acc + w_ref[k] * ext_ref[pl.ds(t0 + off + k, CONV_TB)]
        y_ref[0, pl.ds(t0, CONV_TB)] = acc + bias
        return carry

    lax.fori_loop(0, tc // CONV_TB, group, 0)


def _conv_module(u4, conv_w, conv_b, tc=512):
    b, s = u4.shape[:2]
    nh = tc // HALO
    n_halo_blocks = s // HALO
    return pl.pallas_call(
        _conv_kernel,
        grid=(b, s // tc),
        in_specs=[
            pl.BlockSpec((1, tc, SUB, LANE), lambda i, j: (i, j, 0, 0)),
            pl.BlockSpec((1, HALO, SUB, LANE), lambda i, j: (i, jnp.maximum(j * nh - 1, 0), 0, 0)),
            pl.BlockSpec((1, HALO, SUB, LANE),
                         lambda i, j: (i, jnp.minimum((j + 1) * nh, n_halo_blocks - 1), 0, 0)),
            _const_spec((CONV_W, SUB, LANE)), _const_spec((SUB, LANE)),
        ],
        out_specs=pl.BlockSpec((1, tc, SUB, LANE), lambda i, j: (i, j, 0, 0)),
        out_shape=jax.ShapeDtypeStruct((b, s, SUB, LANE), F32),
        scratch_shapes=[pltpu.VMEM((tc + 2 * HALO, SUB, LANE), F32)],
        compiler_params=pltpu.CompilerParams(dimension_semantics=("parallel", "parallel")),
        name="conv_module",
    )(u4, u4, u4, conv_w.reshape(CONV_W, SUB, LANE), conv_b.reshape(SUB, LANE))


def _attn_kernel(qT_ref, k_ref, vT_ref, o_ref, s_ref, acc_ref, *, tk1, tk2):
    s_len = k_ref.shape[2]
    tq = qT_ref.shape[3]
    qT = qT_ref[0, 0]

    def scores(c, mpart):
        start = pl.multiple_of(c * tk1, tk1)
        s = jnp.dot(k_ref[0, 0, pl.ds(start, tk1), :], qT, preferred_element_type=F32)
        s_ref[pl.ds(start, tk1), :] = s
        return jnp.maximum(mpart, jnp.max(s.reshape(tk1 // 8, 8, tq), axis=0))

    mpart = lax.fori_loop(0, s_len // tk1, scores, jnp.full((8, tq), -jnp.inf, F32))
    m = jnp.max(mpart, axis=0, keepdims=True)

    acc_ref[...] = jnp.zeros(acc_ref.shape, F32)

    def weighted(c, carry):
        start = pl.multiple_of(c * tk2, tk2)
        p = jnp.exp2(s_ref[pl.ds(start, tk2), :] - m).astype(BF16)
        acc_ref[...] += jnp.dot(vT_ref[0, 0, :, pl.ds(start, tk2)], p, preferred_element_type=F32)
        return carry

    lax.fori_loop(0, s_len // tk2, weighted, 0)
    o = acc_ref[0:V_DIM, :] / acc_ref[V_DIM:V_DIM + 1, :]
    o_ref[0] = o.T.astype(o_ref.dtype)


def _attention(qT, k, vT, tq=512, tk1=4096, tk2=4096):
    b, h, _, s = qT.shape
    return pl.pallas_call(
        functools.partial(_attn_kernel, tk1=tk1, tk2=tk2),
        grid=(b, h, s // tq),
        in_specs=[
            pl.BlockSpec((1, 1, QK_PAD, tq), lambda bi, hi, qi: (bi, hi, 0, qi)),
            pl.BlockSpec((1, 1, s, QK_PAD), lambda bi, hi, qi: (bi, hi, 0, 0)),
            pl.BlockSpec((1, 1, VT_ROWS, s), lambda bi, hi, qi: (bi, hi, 0, 0)),
        ],
        out_specs=pl.BlockSpec((1, tq, V_DIM), lambda bi, hi, qi: (bi, qi, hi)),
        out_shape=jax.ShapeDtypeStruct((b, s, h * V_DIM), BF16),
        scratch_shapes=[pltpu.VMEM((s, tq), F32), pltpu.VMEM((VT_ROWS, tq), F32)],
        compiler_params=pltpu.CompilerParams(
            dimension_semantics=("parallel", "parallel", "arbitrary"), vmem_limit_bytes=VMEM_LIMIT),
        name="attention",
    )(qT, k, vT)


def _merge_kernel(x_ref, lng_ref, lnb_ref, a_ref, y_ref, gc_ref, bc_ref, wa_ref, wc_ref, g1_ref, b1_ref, o_ref):
    x0 = _layer_norm(x_ref[...], lng_ref[...], lnb_ref[...])
    yn = _layer_norm(y_ref[...].reshape(y_ref.shape[0], CONV_CH), gc_ref[...], bc_ref[...])
    conv_out = (yn * jax.nn.sigmoid(yn)).astype(BF16)
    mix = (jnp.dot(a_ref[...], wa_ref[...], preferred_element_type=F32)
           + jnp.dot(conv_out, wc_ref[...], preferred_element_type=F32))
    o_ref[...] = _layer_norm(ALPHA * x0 + mix, g1_ref[...], b1_ref[...])


def _merge(x2d, ln_g, ln_b, attn2d, y2d, gc, bc, wa, wc, g1, b1, tm=512):
    t, d = x2d.shape
    row = lambda i: (i, 0)
    return pl.pallas_call(
        _merge_kernel,
        grid=(t // tm,),
        in_specs=[
            pl.BlockSpec((tm, d), row), _const_spec((1, d)), _const_spec((1, d)),
            pl.BlockSpec((tm, MLA_WIDTH), row), pl.BlockSpec((tm, SUB, LANE), lambda i: (i, 0, 0)),
            _const_spec((1, CONV_CH)), _const_spec((1, CONV_CH)),
            _const_spec(wa.shape), _const_spec(wc.shape), _const_spec((1, d)), _const_spec((1, d)),
        ],
        out_specs=pl.BlockSpec((tm, d), row),
        out_shape=jax.ShapeDtypeStruct((t, d), F32),
        compiler_params=pltpu.CompilerParams(
            dimension_semantics=("parallel",), vmem_limit_bytes=VMEM_LIMIT),
        name="merge",
    )(x2d, ln_g.reshape(1, d), ln_b.reshape(1, d), attn2d, y2d, gc.reshape(1, CONV_CH), bc.reshape(1, CONV_CH),
      wa, wc, g1.reshape(1, d), b1.reshape(1, d))


def _ffn_kernel(x_ref, w1_ref, w2_ref, g_ref, b_ref, o_ref, xb_ref):
    j = pl.program_id(1)

    @pl.when(j == 0)
    def _():
        xb_ref[...] = x_ref[...].astype(BF16)
        o_ref[...] = jnp.zeros(o_ref.shape, F32)

    h = jnp.dot(xb_ref[...], w1_ref[...], preferred_element_type=F32)
    h = jnp.maximum(h, 0.0)
    o_ref[...] += jnp.dot((h * h).astype(BF16), w2_ref[...], preferred_element_type=F32)

    @pl.when(j == pl.num_programs(1) - 1)
    def _():
        o_ref[...] = _layer_norm(ALPHA * x_ref[...] + o_ref[...], g_ref[...], b_ref[...])


def _ffn(x1, w1, w2, g2, b2, tm=512, tf=1024):
    t, d = x1.shape
    f = w1.shape[1]
    return pl.pallas_call(
        _ffn_kernel,
        grid=(t // tm, f // tf),
        in_specs=[
            pl.BlockSpec((tm, d), lambda i, j: (i, 0)),
            pl.BlockSpec((d, tf), lambda i, j: (0, j)),
            pl.BlockSpec((tf, d), lambda i, j: (j, 0)),
            _const_spec((1, d)), _const_spec((1, d)),
        ],
        out_specs=pl.BlockSpec((tm, d), lambda i, j: (i, 0)),
        out_shape=jax.ShapeDtypeStruct((t, d), F32),
        scratch_shapes=[pltpu.VMEM((tm, d), BF16)],
        compiler_params=pltpu.CompilerParams(
            dimension_semantics=("parallel", "arbitrary"), vmem_limit_bytes=VMEM_LIMIT),
        name="ffn",
    )(x1, w1, w2, g2.reshape(1, d), b2.reshape(1, d))


def kernel(x, positions, ln_in_g, ln_in_b, w_in, g_cq, w_uq, g_ckv, w_uk, w_uv, conv_w, conv_b,
           g_conv_ln, b_conv_ln, w_out, g_ln1, b_ln1, w_ff1, w_ff2, g_ln2, b_ln2):
    b, s, d = x.shape
    w_in0 = w_in[0]
    o_kr = Q_RANK + KV_RANK
    wc = w_in0[:, :o_kr].astype(BF16)
    wkrT = w_in0[:, o_kr:o_kr + ROPE].T.astype(BF16)
    wconv = w_in0[:, o_kr + ROPE:].astype(BF16)
    wuqT = w_uq[0].T.astype(BF16)
    wuk = w_uk[0].astype(BF16)
    wuvT = w_uv[0].T.astype(BF16)
    wa = w_out[0, :MLA_WIDTH].astype(BF16)
    wcv = w_out[0, MLA_WIDTH:].astype(BF16)
    w1 = w_ff1[0].astype(BF16)
    w2 = w_ff2[0].astype(BF16)

    cosT, sinT = _rope_tables(positions)
    qT, k, vT, u = _in_proj(x, ln_in_g, ln_in_b, wc, wkrT, wconv, g_cq[0], g_ckv[0],
                            wuqT, wuk, wuvT, cosT, sinT)
    y = _conv_module(u, conv_w[0], conv_b[0])
    attn = _attention(qT, k, vT)
    x2d = x.reshape(b * s, d)
    x1 = _merge(x2d, ln_in_g, ln_in_b, attn.reshape(b * s, MLA_WIDTH), y.reshape(b * s, SUB, LANE),
                g_conv_ln[0], b_conv_ln[0], wa, wcv, g_ln1[0], b_ln1[0])
    out = _ffn(x1, w1, w2, g_ln2[0], b_ln2[0])
    return out.reshape(b, s, d)
```

```python
import functools
import math

import jax
import jax.numpy as jnp
from jax import lax
from jax.experimental import pallas as pl
from jax.experimental.pallas import tpu as pltpu

F32 = jnp.float32
BF16 = jnp.bfloat16

D_MODEL = 2048
HEADS = 8
NOPE = 128
ROPE = 64
HALF = ROPE // 2
QK = NOPE + ROPE
QK_PAD = 256
V_DIM = 128
VT_ROWS = V_DIM + 16
Q_RANK = 512
KV_RANK = 512
MLA_WIDTH = HEADS * V_DIM
CONV_CH = D_MODEL - MLA_WIDTH
CONV_W = 31
CONV_PAD = CONV_W // 2
HALO = 16
D_FF = 4 * D_MODEL
ROPE_BASE = 10000.0
LN_EPS = 1e-5
RMS_EPS = 1e-6
ALPHA = 2.0 ** 0.25
Q_SCALE = (QK ** -0.5) * math.log2(math.e)

VMEM_LIMIT = 56 * 1024 * 1024

NT_DIMS = (((1,), (1,)), ((), ()))


def _const_spec(shape):
    nd = len(shape)
    return pl.BlockSpec(shape, lambda *_: (0,) * nd, pipeline_mode=pl.Buffered(1))


def _layer_norm(x, g, b):
    mu = jnp.mean(x, axis=-1, keepdims=True)
    xc = x - mu
    var = jnp.mean(xc * xc, axis=-1, keepdims=True)
    return xc * lax.rsqrt(var + LN_EPS) * g + b


def _rope_kernel(pos_ref, invf_ref, cos_ref, sin_ref):
    ang = invf_ref[...] * pos_ref[0].astype(F32)
    cos_ref[0] = jnp.cos(ang)
    sin_ref[0] = jnp.sin(ang)


def _rope_tables(positions, ts=2048):
    b, s = positions.shape
    inv_freq = ROPE_BASE ** (-jnp.arange(HALF, dtype=F32) * (2.0 / ROPE))
    out = jax.ShapeDtypeStruct((b, HALF, s), F32)
    return pl.pallas_call(
        _rope_kernel,
        grid=(b, s // ts),
        in_specs=[pl.BlockSpec((1, 1, ts), lambda i, j: (i, 0, j)),
                  pl.BlockSpec((HALF, 1), lambda i, j: (0, 0))],
        out_specs=[pl.BlockSpec((1, HALF, ts), lambda i, j: (i, 0, j))] * 2,
        out_shape=[out, out],
        name="rope_tables",
    )(positions.reshape(b, 1, s), inv_freq.reshape(HALF, 1))


def _inproj_kernel(x_ref, lng_ref, lnb_ref, wc_ref, wkrT_ref, wconv_ref, gcq_ref, gckv_ref,
                   wuqT_ref, wuk_ref, wuvT_ref, cos_ref, sin_ref,
                   qT_ref, k_ref, vT_ref, u_ref):
    tm = x_ref.shape[1]
    xb = _layer_norm(x_ref[0], lng_ref[...], lnb_ref[...]).astype(BF16)
    cos = cos_ref[0]
    sin = sin_ref[0]

    c = jnp.dot(xb, wc_ref[...], preferred_element_type=F32)
    c_q = c[:, :Q_RANK]
    c_kv = c[:, Q_RANK:]
    rq = lax.rsqrt(jnp.mean(c_q * c_q, axis=-1, keepdims=True) + RMS_EPS) * Q_SCALE
    cqn = (c_q * rq * gcq_ref[...]).astype(BF16)
    rkv = lax.rsqrt(jnp.mean(c_kv * c_kv, axis=-1, keepdims=True) + RMS_EPS)
    ckvn = (c_kv * rkv * gckv_ref[...]).astype(BF16)

    qT = lax.dot_general(wuqT_ref[...], cqn, NT_DIMS, preferred_element_type=F32)
    zeros_q = jnp.zeros((QK_PAD - QK, tm), BF16)
    for h in range(HEADS):
        base = h * QK
        r1 = qT[base + NOPE:base + NOPE + HALF]
        r2 = qT[base + NOPE + HALF:base + QK]
        qT_ref[0, h, 0:NOPE, :] = qT[base:base + NOPE].astype(BF16)
        qT_ref[0, h, NOPE:NOPE + HALF, :] = (r1 * cos - r2 * sin).astype(BF16)
        qT_ref[0, h, NOPE + HALF:QK, :] = (r2 * cos + r1 * sin).astype(BF16)
        qT_ref[0, h, QK:QK_PAD, :] = zeros_q

    krT = lax.dot_general(wkrT_ref[...], xb, NT_DIMS, preferred_element_type=F32)
    k1 = krT[:HALF]
    k2 = krT[HALF:]
    kr_fullT = jnp.concatenate(
        [k1 * cos - k2 * sin, k2 * cos + k1 * sin, jnp.zeros((QK_PAD - QK, tm), F32)], axis=0)
    kr = kr_fullT.T.astype(BF16)

    kn = jnp.dot(ckvn, wuk_ref[...], preferred_element_type=F32)
    vT = lax.dot_general(wuvT_ref[...], ckvn, NT_DIMS, preferred_element_type=F32)
    row = lax.broadcasted_iota(jnp.int32, (VT_ROWS - V_DIM, tm), 0)
    ones_row = jnp.where(row == 0, 1.0, 0.0).astype(BF16)
    for h in range(HEADS):
        k_ref[0, h, :, 0:NOPE] = kn[:, h * NOPE:(h + 1) * NOPE].astype(BF16)
        k_ref[0, h, :, NOPE:QK_PAD] = kr
        vT_ref[0, h, 0:V_DIM, :] = vT[h * V_DIM:(h + 1) * V_DIM].astype(BF16)
        vT_ref[0, h, V_DIM:VT_ROWS, :] = ones_row

    ag = jnp.dot(xb, wconv_ref[...], preferred_element_type=F32)
    u = ag[:, :CONV_CH] * jax.nn.sigmoid(ag[:, CONV_CH:])
    u_ref[0] = u.reshape(tm, SUB, LANE)


def _in_proj(x, ln_g, ln_b, wc, wkrT, wconv, g_cq, g_ckv, wuqT, wuk, wuvT, cosT, sinT, tm=512):
    b, s, d = x.shape
    grid = (b, s // tm)
    out_shape = [
        jax.ShapeDtypeStruct((b, HEADS, QK_PAD, s), BF16),
        jax.ShapeDtypeStruct((b, HEADS, s, QK_PAD), BF16),
        jax.ShapeDtypeStruct((b, HEADS, VT_ROWS, s), BF16),
        jax.ShapeDtypeStruct((b, s, SUB, LANE), F32),
    ]
    in_specs = [
        pl.BlockSpec((1, tm, d), lambda i, j: (i, j, 0)),
        _const_spec((1, d)), _const_spec((1, d)),
        _const_spec(wc.shape), _const_spec(wkrT.shape), _const_spec(wconv.shape),
        _const_spec((1, Q_RANK)), _const_spec((1, KV_RANK)),
        _const_spec(wuqT.shape), _const_spec(wuk.shape), _const_spec(wuvT.shape),
        pl.BlockSpec((1, HALF, tm), lambda i, j: (i, 0, j)),
        pl.BlockSpec((1, HALF, tm), lambda i, j: (i, 0, j)),
    ]
    out_specs = [
        pl.BlockSpec((1, HEADS, QK_PAD, tm), lambda i, j: (i, 0, 0, j)),
        pl.BlockSpec((1, HEADS, tm, QK_PAD), lambda i, j: (i, 0, j, 0)),
        pl.BlockSpec((1, HEADS, VT_ROWS, tm), lambda i, j: (i, 0, 0, j)),
        pl.BlockSpec((1, tm, SUB, LANE), lambda i, j: (i, j, 0, 0)),
    ]
    return pl.pallas_call(
        _inproj_kernel, grid=grid, in_specs=in_specs, out_specs=out_specs, out_shape=out_shape,
        compiler_params=pltpu.CompilerParams(
            dimension_semantics=("parallel", "parallel"), vmem_limit_bytes=VMEM_LIMIT),
        name="in_proj",
    )(x, ln_g.reshape(1, d), ln_b.reshape(1, d), wc, wkrT, wconv,
      g_cq.reshape(1, Q_RANK), g_ckv.reshape(1, KV_RANK), wuqT, wuk, wuvT, cosT, sinT)


CONV_TB = 16
SUB = 8
LANE = 128


def _conv_kernel(u_ref, prev_ref, next_ref, w_ref, b_ref, y_ref, ext_ref):
    tc = u_ref.shape[1]
    i = pl.program_id(1)
    last = pl.num_programs(1) - 1
    ext_ref[0:HALO] = jnp.where(i > 0, prev_ref[0], 0.0)
    ext_ref[HALO:HALO + tc] = u_ref[0]
    ext_ref[HALO + tc:HALO + tc + HALO] = jnp.where(i < last, next_ref[0], 0.0)
    off = HALO - CONV_PAD
    bias = b_ref[...]

    def group(gi, carry):
        t0 = pl.multiple_of(gi * CONV_TB, CONV_TB)
        acc = jnp.zeros((CONV_TB, SUB, LANE), F32)
        for k in range(CONV_W):
            acc = acc + w_ref[k] * ext_ref[pl.ds(t0 + off + k, CONV_TB)]
        y_ref[0, pl.ds(t0, CONV_TB)] = acc + bias
        return carry

    lax.fori_loop(0, tc // CONV_TB, group, 0)


def _conv_module(u4, conv_w, conv_b, tc=512):
    b, s = u4.shape[:2]
    nh = tc // HALO
    n_halo_blocks = s // HALO
    return pl.pallas_call(
        _conv_kernel,
        grid=(b, s // tc),
        in_specs=[
            pl.BlockSpec((1, tc, SUB, LANE), lambda i, j: (i, j, 0, 0)),
            pl.BlockSpec((1, HALO, SUB, LANE), lambda i, j: (i, jnp.maximum(j * nh - 1, 0), 0, 0)),
            pl.BlockSpec((1, HALO, SUB, LANE),
                         lambda i, j: (i, jnp.minimum((j + 1) * nh, n_halo_blocks - 1), 0, 0)),
            _const_spec((CONV_W, SUB, LANE)), _const_spec((SUB, LANE)),
        ],
        out_specs=pl.BlockSpec((1, tc, SUB, LANE), lambda i, j: (i, j, 0, 0)),
        out_shape=jax.ShapeDtypeStruct((b, s, SUB, LANE), F32),
        scratch_shapes=[pltpu.VMEM((tc + 2 * HALO, SUB, LANE), F32)],
        compiler_params=pltpu.CompilerParams(dimension_semantics=("parallel", "parallel")),
        name="conv_module",
    )(u4, u4, u4, conv_w.reshape(CONV_W, SUB, LANE), conv_b.reshape(SUB, LANE))


def _attn_kernel(qT_ref, k_ref, vT_ref, o_ref, sa_ref, sb_ref, acc_ref, *, tq, tk, ts):
    s_len = k_ref.shape[2]
    nq = s_len // tq
    nchunk = s_len // tk
    neg = jnp.full((SUB, tq), -jnp.inf, F32)

    def q_block(q):
        return qT_ref[0, 0, :, pl.ds(pl.multiple_of(q * tq, tq), tq)]

    def score_chunk(start, qT, s_ref, mpart):
        s = jnp.dot(k_ref[0, 0, pl.ds(start, ts), :], qT, preferred_element_type=F32)
        s_ref[pl.ds(start, ts), :] = s
        return jnp.maximum(mpart, jnp.max(s.reshape(ts // SUB, SUB, tq), axis=0))

    def weight_chunk(start, s_ref, m):
        p = jnp.exp2(s_ref[pl.ds(start, ts), :] - m).astype(BF16)
        acc_ref[...] += jnp.dot(vT_ref[0, 0, :, pl.ds(start, ts)], p, preferred_element_type=F32)

    def finish(q):
        o = acc_ref[0:V_DIM, :] / acc_ref[V_DIM:V_DIM + 1, :]
        o_ref[0, pl.ds(pl.multiple_of(q * tq, tq), tq), :] = o.T.astype(o_ref.dtype)

    def sweep(q_next, s_next_ref, q_cur, s_cur_ref, mpart_cur):
        qT = None if q_next is None else q_block(q_next)
        if q_cur is not None:
            m = jnp.max(mpart_cur, axis=0, keepdims=True)
            acc_ref[...] = jnp.zeros(acc_ref.shape, F32)

        def body(c, mpart):
            for j in range(tk // ts):
                start = pl.multiple_of(c * tk + j * ts, ts)
                if q_cur is not None:
                    weight_chunk(start, s_cur_ref, m)
                if q_next is not None:
                    mpart = score_chunk(start, qT, s_next_ref, mpart)
            return mpart

        mpart_next = lax.fori_loop(0, nchunk, body, neg)
        if q_cur is not None:
            finish(q_cur)
        return mpart_next

    mpart_a = sweep(0, sa_ref, None, None, None)

    def pair(i, mpart_a):
        q = 2 * i
        mpart_b = sweep(q + 1, sb_ref, q, sa_ref, mpart_a)
        return sweep(q + 2, sa_ref, q + 1, sb_ref, mpart_b)

    mpart_a = lax.fori_loop(0, nq // 2 - 1, pair, mpart_a)
    mpart_b = sweep(nq - 1, sb_ref, nq - 2, sa_ref, mpart_a)
    sweep(None, None, nq - 1, sb_ref, mpart_b)


def _attention(qT, k, vT, tq=512, tk=8192, ts=512):
    b, h, _, s = qT.shape
    return pl.pallas_call(
        functools.partial(_attn_kernel, tq=tq, tk=tk, ts=ts),
        grid=(b, h),
        in_specs=[
            pl.BlockSpec((1, 1, QK_PAD, s), lambda bi, hi: (bi, hi, 0, 0), pipeline_mode=pl.Buffered(1)),
            pl.BlockSpec((1, 1, s, QK_PAD), lambda bi, hi: (bi, hi, 0, 0)),
            pl.BlockSpec((1, 1, VT_ROWS, s), lambda bi, hi: (bi, hi, 0, 0)),
        ],
        out_specs=pl.BlockSpec((1, s, V_DIM), lambda bi, hi: (bi, 0, hi)),
        out_shape=jax.ShapeDtypeStruct((b, s, h * V_DIM), BF16),
        scratch_shapes=[pltpu.VMEM((s, tq), F32), pltpu.VMEM((s, tq), F32), pltpu.VMEM((VT_ROWS, tq), F32)],
        compiler_params=pltpu.CompilerParams(
            dimension_semantics=("parallel", "parallel"), vmem_limit_bytes=VMEM_LIMIT),
        name="attention",
    )(qT, k, vT)


def _merge_kernel(x_ref, lng_ref, lnb_ref, a_ref, y_ref, gc_ref, bc_ref, wa_ref, wc_ref, g1_ref, b1_ref, o_ref):
    x0 = _layer_norm(x_ref[...], lng_ref[...], lnb_ref[...])
    yn = _layer_norm(y_ref[...].reshape(y_ref.shape[0], CONV_CH), gc_ref[...], bc_ref[...])
    conv_out = (yn * jax.nn.sigmoid(yn)).astype(BF16)
    mix = (jnp.dot(a_ref[...], wa_ref[...], preferred_element_type=F32)
           + jnp.dot(conv_out, wc_ref[...], preferred_element_type=F32))
    o_ref[...] = _layer_norm(ALPHA * x0 + mix, g1_ref[...], b1_ref[...])


def _merge(x2d, ln_g, ln_b, attn2d, y2d, gc, bc, wa, wc, g1, b1, tm=512):
    t, d = x2d.shape
    row = lambda i: (i, 0)
    return pl.pallas_call(
        _merge_kernel,
        grid=(t // tm,),
        in_specs=[
            pl.BlockSpec((tm, d), row), _const_spec((1, d)), _const_spec((1, d)),
            pl.BlockSpec((tm, MLA_WIDTH), row), pl.BlockSpec((tm, SUB, LANE), lambda i: (i, 0, 0)),
            _const_spec((1, CONV_CH)), _const_spec((1, CONV_CH)),
            _const_spec(wa.shape), _const_spec(wc.shape), _const_spec((1, d)), _const_spec((1, d)),
        ],
        out_specs=pl.BlockSpec((tm, d), row),
        out_shape=jax.ShapeDtypeStruct((t, d), F32),
        compiler_params=pltpu.CompilerParams(
            dimension_semantics=("parallel",), vmem_limit_bytes=VMEM_LIMIT),
        name="merge",
    )(x2d, ln_g.reshape(1, d), ln_b.reshape(1, d), attn2d, y2d, gc.reshape(1, CONV_CH), bc.reshape(1, CONV_CH),
      wa, wc, g1.reshape(1, d), b1.reshape(1, d))


def _ffn_kernel(x_ref, w1_ref, w2_ref, g_ref, b_ref, o_ref, xb_ref):
    j = pl.program_id(1)

    @pl.when(j == 0)
    def _():
        xb_ref[...] = x_ref[...].astype(BF16)
        o_ref[...] = jnp.zeros(o_ref.shape, F32)

    h = jnp.dot(xb_ref[...], w1_ref[...], preferred_element_type=F32)
    h = jnp.maximum(h, 0.0)
    o_ref[...] += jnp.dot((h * h).astype(BF16), w2_ref[...], preferred_element_type=F32)

    @pl.when(j == pl.num_programs(1) - 1)
    def _():
        o_ref[...] = _layer_norm(ALPHA * x_ref[...] + o_ref[...], g_ref[...], b_ref[...])


def _ffn(x1, w1, w2, g2, b2, tm=512, tf=1024):
    t, d = x1.shape
    f = w1.shape[1]
    return pl.pallas_call(
        _ffn_kernel,
        grid=(t // tm, f // tf),
        in_specs=[
            pl.BlockSpec((tm, d), lambda i, j: (i, 0)),
            pl.BlockSpec((d, tf), lambda i, j: (0, j)),
            pl.BlockSpec((tf, d), lambda i, j: (j, 0)),
            _const_spec((1, d)), _const_spec((1, d)),
        ],
        out_specs=pl.BlockSpec((tm, d), lambda i, j: (i, 0)),
        out_shape=jax.ShapeDtypeStruct((t, d), F32),
        scratch_shapes=[pltpu.VMEM((tm, d), BF16)],
        compiler_params=pltpu.CompilerParams(
            dimension_semantics=("parallel", "arbitrary"), vmem_limit_bytes=VMEM_LIMIT),
        name="ffn",
    )(x1, w1, w2, g2.reshape(1, d), b2.reshape(1, d))


def kernel(x, positions, ln_in_g, ln_in_b, w_in, g_cq, w_uq, g_ckv, w_uk, w_uv, conv_w, conv_b,
           g_conv_ln, b_conv_ln, w_out, g_ln1, b_ln1, w_ff1, w_ff2, g_ln2, b_ln2):
    b, s, d = x.shape
    w_in0 = w_in[0]
    o_kr = Q_RANK + KV_RANK
    wc = w_in0[:, :o_kr].astype(BF16)
    wkrT = w_in0[:, o_kr:o_kr + ROPE].T.astype(BF16)
    wconv = w_in0[:, o_kr + ROPE:].astype(BF16)
    wuqT = w_uq[0].T.astype(BF16)
    wuk = w_uk[0].astype(BF16)
    wuvT = w_uv[0].T.astype(BF16)
    wa = w_out[0, :MLA_WIDTH].astype(BF16)
    wcv = w_out[0, MLA_WIDTH:].astype(BF16)
    w1 = w_ff1[0].astype(BF16)
    w2 = w_ff2[0].astype(BF16)

    cosT, sinT = _rope_tables(positions)
    qT, k, vT, u = _in_proj(x, ln_in_g, ln_in_b, wc, wkrT, wconv, g_cq[0], g_ckv[0],
                            wuqT, wuk, wuvT, cosT, sinT)
    y = _conv_module(u, conv_w[0], conv_b[0])
    attn = _attention(qT, k, vT)
    x2d = x.reshape(b * s, d)
    x1 = _merge(x2d, ln_in_g, ln_in_b, attn.reshape(b * s, MLA_WIDTH), y.reshape(b * s, SUB, LANE),
                g_conv_ln[0], b_conv_ln[0], wa, wcv, g_ln1[0], b_ln1[0])
    out = _ffn(x1, w1, w2, g_ln2[0], b_ln2[0])
    return out.reshape(b, s, d)
```

```python
import functools
import math

import jax
import jax.numpy as jnp
from jax import lax
from jax.experimental import pallas as pl
from jax.experimental.pallas import tpu as pltpu

F32 = jnp.float32
BF16 = jnp.bfloat16

D_MODEL = 2048
HEADS = 8
NOPE = 128
ROPE = 64
HALF = ROPE // 2
QK = NOPE + ROPE
QK_PAD = 256
V_DIM = 128
VT_ROWS = V_DIM + 16
Q_RANK = 512
KV_RANK = 512
MLA_WIDTH = HEADS * V_DIM
CONV_CH = D_MODEL - MLA_WIDTH
CONV_W = 31
CONV_PAD = CONV_W // 2
HALO = 16
D_FF = 4 * D_MODEL
ROPE_BASE = 10000.0
LN_EPS = 1e-5
RMS_EPS = 1e-6
ALPHA = 2.0 ** 0.25
Q_SCALE = (QK ** -0.5) * math.log2(math.e)

SUB = 8
LANE = 128
VMEM_LIMIT = 56 * 1024 * 1024

NT_DIMS = (((1,), (1,)), ((), ()))


def _const_spec(shape):
    nd = len(shape)
    return pl.BlockSpec(shape, lambda *_: (0,) * nd, pipeline_mode=pl.Buffered(1))


def _layer_norm(x, g, b):
    mu = jnp.mean(x, axis=-1, keepdims=True)
    xc = x - mu
    var = jnp.mean(xc * xc, axis=-1, keepdims=True)
    return xc * lax.rsqrt(var + LN_EPS) * g + b


def _rope_kernel(pos_ref, invf_ref, cos_ref, sin_ref):
    ang = invf_ref[...] * pos_ref[0].astype(F32)
    cos_ref[0] = jnp.cos(ang)
    sin_ref[0] = jnp.sin(ang)


def _rope_tables(positions, ts=2048):
    b, s = positions.shape
    inv_freq = ROPE_BASE ** (-jnp.arange(HALF, dtype=F32) * (2.0 / ROPE))
    out = jax.ShapeDtypeStruct((b, HALF, s), F32)
    return pl.pallas_call(
        _rope_kernel,
        grid=(b, s // ts),
        in_specs=[pl.BlockSpec((1, 1, ts), lambda i, j: (i, 0, j)),
                  pl.BlockSpec((HALF, 1), lambda i, j: (0, 0))],
        out_specs=[pl.BlockSpec((1, HALF, ts), lambda i, j: (i, 0, j))] * 2,
        out_shape=[out, out],
        name="rope_tables",
    )(positions.reshape(b, 1, s), inv_freq.reshape(HALF, 1))


def _inproj_kernel(x_ref, lng_ref, lnb_ref, wc_ref, wkrT_ref, wconv_ref, gcq_ref, gckv_ref,
                   wuqT_ref, wuk_ref, wuvT_ref, cos_ref, sin_ref,
                   qT_ref, k_ref, vT_ref, u_ref):
    tm = x_ref.shape[1]
    xb = _layer_norm(x_ref[0], lng_ref[...], lnb_ref[...]).astype(BF16)
    cos = cos_ref[0]
    sin = sin_ref[0]

    c = jnp.dot(xb, wc_ref[...], preferred_element_type=F32)
    c_q = c[:, :Q_RANK]
    c_kv = c[:, Q_RANK:]
    rq = lax.rsqrt(jnp.mean(c_q * c_q, axis=-1, keepdims=True) + RMS_EPS) * Q_SCALE
    cqn = (c_q * rq * gcq_ref[...]).astype(BF16)
    rkv = lax.rsqrt(jnp.mean(c_kv * c_kv, axis=-1, keepdims=True) + RMS_EPS)
    ckvn = (c_kv * rkv * gckv_ref[...]).astype(BF16)

    qT = lax.dot_general(wuqT_ref[...], cqn, NT_DIMS, preferred_element_type=F32)
    zeros_q = jnp.zeros((QK_PAD - QK, tm), BF16)
    for h in range(HEADS):
        base = h * QK
        r1 = qT[base + NOPE:base + NOPE + HALF]
        r2 = qT[base + NOPE + HALF:base + QK]
        qT_ref[0, h, 0:NOPE, :] = qT[base:base + NOPE].astype(BF16)
        qT_ref[0, h, NOPE:NOPE + HALF, :] = (r1 * cos - r2 * sin).astype(BF16)
        qT_ref[0, h, NOPE + HALF:QK, :] = (r2 * cos + r1 * sin).astype(BF16)
        qT_ref[0, h, QK:QK_PAD, :] = zeros_q

    krT = lax.dot_general(wkrT_ref[...], xb, NT_DIMS, preferred_element_type=F32)
    k1 = krT[:HALF]
    k2 = krT[HALF:]
    kr_fullT = jnp.concatenate(
        [k1 * cos - k2 * sin, k2 * cos + k1 * sin, jnp.zeros((QK_PAD - QK, tm), F32)], axis=0)
    kr = kr_fullT.T.astype(BF16)

    kn = jnp.dot(ckvn, wuk_ref[...], preferred_element_type=F32)
    vT = lax.dot_general(wuvT_ref[...], ckvn, NT_DIMS, preferred_element_type=F32)
    row = lax.broadcasted_iota(jnp.int32, (VT_ROWS - V_DIM, tm), 0)
    ones_row = jnp.where(row == 0, 1.0, 0.0).astype(BF16)
    for h in range(HEADS):
        k_ref[0, h, :, 0:NOPE] = kn[:, h * NOPE:(h + 1) * NOPE].astype(BF16)
        k_ref[0, h, :, NOPE:QK_PAD] = kr
        vT_ref[0, h, 0:V_DIM, :] = vT[h * V_DIM:(h + 1) * V_DIM].astype(BF16)
        vT_ref[0, h, V_DIM:VT_ROWS, :] = ones_row

    ag = jnp.dot(xb, wconv_ref[...], preferred_element_type=F32)
    u = ag[:, :CONV_CH] * jax.nn.sigmoid(ag[:, CONV_CH:])
    u_ref[0] = u.reshape(tm, SUB, LANE)


def _in_proj(x, ln_g, ln_b, wc, wkrT, wconv, g_cq, g_ckv, wuqT, wuk, wuvT, cosT, sinT, tm=512):
    b, s, d = x.shape
    grid = (b, s // tm)
    out_shape = [
        jax.ShapeDtypeStruct((b, HEADS, QK_PAD, s), BF16),
        jax.ShapeDtypeStruct((b, HEADS, s, QK_PAD), BF16),
        jax.ShapeDtypeStruct((b, HEADS, VT_ROWS, s), BF16),
        jax.ShapeDtypeStruct((b, s, SUB, LANE), F32),
    ]
    in_specs = [
        pl.BlockSpec((1, tm, d), lambda i, j: (i, j, 0)),
        _const_spec((1, d)), _const_spec((1, d)),
        _const_spec(wc.shape), _const_spec(wkrT.shape), _const_spec(wconv.shape),
        _const_spec((1, Q_RANK)), _const_spec((1, KV_RANK)),
        _const_spec(wuqT.shape), _const_spec(wuk.shape), _const_spec(wuvT.shape),
        pl.BlockSpec((1, HALF, tm), lambda i, j: (i, 0, j)),
        pl.BlockSpec((1, HALF, tm), lambda i, j: (i, 0, j)),
    ]
    out_specs = [
        pl.BlockSpec((1, HEADS, QK_PAD, tm), lambda i, j: (i, 0, 0, j)),
        pl.BlockSpec((1, HEADS, tm, QK_PAD), lambda i, j: (i, 0, j, 0)),
        pl.BlockSpec((1, HEADS, VT_ROWS, tm), lambda i, j: (i, 0, 0, j)),
        pl.BlockSpec((1, tm, SUB, LANE), lambda i, j: (i, j, 0, 0)),
    ]
    return pl.pallas_call(
        _inproj_kernel, grid=grid, in_specs=in_specs, out_specs=out_specs, out_shape=out_shape,
        compiler_params=pltpu.CompilerParams(
            dimension_semantics=("parallel", "parallel"), vmem_limit_bytes=VMEM_LIMIT),
        name="in_proj",
    )(x, ln_g.reshape(1, d), ln_b.reshape(1, d), wc, wkrT, wconv,
      g_cq.reshape(1, Q_RANK), g_ckv.reshape(1, KV_RANK), wuqT, wuk, wuvT, cosT, sinT)


CONV_TB = 16


def _conv_kernel(u_ref, prev_ref, next_ref, w_ref, b_ref, y_ref, ext_ref):
    tc = u_ref.shape[1]
    i = pl.program_id(1)
    last = pl.num_programs(1) - 1
    ext_ref[0:HALO] = jnp.where(i > 0, prev_ref[0], 0.0)
    ext_ref[HALO:HALO + tc] = u_ref[0]
    ext_ref[HALO + tc:HALO + tc + HALO] = jnp.where(i < last, next_ref[0], 0.0)
    off = HALO - CONV_PAD
    bias = b_ref[...]

    for t0 in range(0, tc, CONV_TB):
        acc = jnp.zeros((CONV_TB, SUB, LANE), F32)
        for k in range(CONV_W):
            acc = acc + w_ref[k] * ext_ref[t0 + off + k:t0 + off + k + CONV_TB]
        y_ref[0, t0:t0 + CONV_TB] = acc + bias


def _conv_module(u4, conv_w, conv_b, tc=512):
    b, s = u4.shape[:2]
    nh = tc // HALO
    n_halo_blocks = s // HALO
    return pl.pallas_call(
        _conv_kernel,
        grid=(b, s // tc),
        in_specs=[
            pl.BlockSpec((1, tc, SUB, LANE), lambda i, j: (i, j, 0, 0)),
            pl.BlockSpec((1, HALO, SUB, LANE), lambda i, j: (i, jnp.maximum(j * nh - 1, 0), 0, 0)),
            pl.BlockSpec((1, HALO, SUB, LANE),
                         lambda i, j: (i, jnp.minimum((j + 1) * nh, n_halo_blocks - 1), 0, 0)),
            _const_spec((CONV_W, SUB, LANE)), _const_spec((SUB, LANE)),
        ],
        out_specs=pl.BlockSpec((1, tc, SUB, LANE), lambda i, j: (i, j, 0, 0)),
        out_shape=jax.ShapeDtypeStruct((b, s, SUB, LANE), F32),
        scratch_shapes=[pltpu.VMEM((tc + 2 * HALO, SUB, LANE), F32)],
        compiler_params=pltpu.CompilerParams(dimension_semantics=("parallel", "parallel")),
        name="conv_module",
    )(u4, u4, u4, conv_w.reshape(CONV_W, SUB, LANE), conv_b.reshape(SUB, LANE))


def _attn_kernel(qT_ref, k_ref, vT_ref, o_ref, sa_ref, sb_ref, acc_ref, *, tq, tk, ts):
    s_len = k_ref.shape[2]
    nq = s_len // tq
    nchunk = s_len // tk
    neg = jnp.full((SUB, tq), -jnp.inf, F32)

    def q_block(q):
        return qT_ref[0, 0, :, pl.ds(pl.multiple_of(q * tq, tq), tq)]

    def score_chunk(start, qT, s_ref, mpart):
        s = jnp.dot(k_ref[0, 0, pl.ds(start, ts), :], qT, preferred_element_type=F32)
        s_ref[pl.ds(start, ts), :] = s
        return jnp.maximum(mpart, jnp.max(s.reshape(ts // SUB, SUB, tq), axis=0))

    def weight_chunk(start, s_ref, m):
        p = jnp.exp2(s_ref[pl.ds(start, ts), :] - m).astype(BF16)
        acc_ref[...] += jnp.dot(vT_ref[0, 0, :, pl.ds(start, ts)], p, preferred_element_type=F32)

    def finish(q):
        o = acc_ref[0:V_DIM, :] / acc_ref[V_DIM:V_DIM + 1, :]
        o_ref[0, pl.ds(pl.multiple_of(q * tq, tq), tq), :] = o.T.astype(o_ref.dtype)

    def sweep(q_next, s_next_ref, q_cur, s_cur_ref, mpart_cur):
        qT = None if q_next is None else q_block(q_next)
        if q_cur is not None:
            m = jnp.max(mpart_cur, axis=0, keepdims=True)
            acc_ref[...] = jnp.zeros(acc_ref.shape, F32)

        def body(c, mpart):
            for j in range(tk // ts):
                start = pl.multiple_of(c * tk + j * ts, ts)
                if q_cur is not None:
                    weight_chunk(start, s_cur_ref, m)
                if q_next is not None:
                    mpart = score_chunk(start, qT, s_next_ref, mpart)
            return mpart

        mpart_next = lax.fori_loop(0, nchunk, body, neg)
        if q_cur is not None:
            finish(q_cur)
        return mpart_next

    mpart_a = sweep(0, sa_ref, None, None, None)

    def pair(i, mpart_a):
        q = 2 * i
        mpart_b = sweep(q + 1, sb_ref, q, sa_ref, mpart_a)
        return sweep(q + 2, sa_ref, q + 1, sb_ref, mpart_b)

    mpart_a = lax.fori_loop(0, nq // 2 - 1, pair, mpart_a)
    mpart_b = sweep(nq - 1, sb_ref, nq - 2, sa_ref, mpart_a)
    sweep(None, None, nq - 1, sb_ref, mpart_b)


def _attention(qT, k, vT, tq=512, tk=8192, ts=512):
    b, h, _, s = qT.shape
    return pl.pallas_call(
        functools.partial(_attn_kernel, tq=tq, tk=tk, ts=ts),
        grid=(b, h),
        in_specs=[
            pl.BlockSpec((1, 1, QK_PAD, s), lambda bi, hi: (bi, hi, 0, 0), pipeline_mode=pl.Buffered(1)),
            pl.BlockSpec((1, 1, s, QK_PAD), lambda bi, hi: (bi, hi, 0, 0)),
            pl.BlockSpec((1, 1, VT_ROWS, s), lambda bi, hi: (bi, hi, 0, 0)),
        ],
        out_specs=pl.BlockSpec((1, s, V_DIM), lambda bi, hi: (bi, 0, hi)),
        out_shape=jax.ShapeDtypeStruct((b, s, h * V_DIM), BF16),
        scratch_shapes=[pltpu.VMEM((s, tq), F32), pltpu.VMEM((s, tq), F32), pltpu.VMEM((VT_ROWS, tq), F32)],
        compiler_params=pltpu.CompilerParams(
            dimension_semantics=("parallel", "parallel"), vmem_limit_bytes=VMEM_LIMIT),
        name="attention",
    )(qT, k, vT)


def _merge_kernel(x_ref, lng_ref, lnb_ref, a_ref, y_ref, gc_ref, bc_ref, wa_ref, wc_ref, g1_ref, b1_ref, o_ref):
    x0 = _layer_norm(x_ref[...], lng_ref[...], lnb_ref[...])
    yn = _layer_norm(y_ref[...].reshape(y_ref.shape[0], CONV_CH), gc_ref[...], bc_ref[...])
    conv_out = (yn * jax.nn.sigmoid(yn)).astype(BF16)
    mix = (jnp.dot(a_ref[...], wa_ref[...], preferred_element_type=F32)
           + jnp.dot(conv_out, wc_ref[...], preferred_element_type=F32))
    o_ref[...] = _layer_norm(ALPHA * x0 + mix, g1_ref[...], b1_ref[...])


def _merge(x2d, ln_g, ln_b, attn2d, y2d, gc, bc, wa, wc, g1, b1, tm=512):
    t, d = x2d.shape
    row = lambda i: (i, 0)
    return pl.pallas_call(
        _merge_kernel,
        grid=(t // tm,),
        in_specs=[
            pl.BlockSpec((tm, d), row), _const_spec((1, d)), _const_spec((1, d)),
            pl.BlockSpec((tm, MLA_WIDTH), row), pl.BlockSpec((tm, SUB, LANE), lambda i: (i, 0, 0)),
            _const_spec((1, CONV_CH)), _const_spec((1, CONV_CH)),
            _const_spec(wa.shape), _const_spec(wc.shape), _const_spec((1, d)), _const_spec((1, d)),
        ],
        out_specs=pl.BlockSpec((tm, d), row),
        out_shape=jax.ShapeDtypeStruct((t, d), F32),
        compiler_params=pltpu.CompilerParams(
            dimension_semantics=("parallel",), vmem_limit_bytes=VMEM_LIMIT),
        name="merge",
    )(x2d, ln_g.reshape(1, d), ln_b.reshape(1, d), attn2d, y2d, gc.reshape(1, CONV_CH), bc.reshape(1, CONV_CH),
      wa, wc, g1.reshape(1, d), b1.reshape(1, d))


def _ffn_kernel(x_ref, w1_ref, w2_ref, g_ref, b_ref, o_ref, xb_ref):
    j = pl.program_id(1)

    @pl.when(j == 0)
    def _():
        xb_ref[...] = x_ref[...].astype(BF16)
        o_ref[...] = jnp.zeros(o_ref.shape, F32)

    h = jnp.dot(xb_ref[...], w1_ref[...], preferred_element_type=F32)
    h = jnp.maximum(h, 0.0)
    o_ref[...] += jnp.dot((h * h).astype(BF16), w2_ref[...], preferred_element_type=F32)

    @pl.when(j == pl.num_programs(1) - 1)
    def _():
        o_ref[...] = _layer_norm(ALPHA * x_ref[...] + o_ref[...], g_ref[...], b_ref[...])


def _ffn(x1, w1, w2, g2, b2, tm=512, tf=1024):
    t, d = x1.shape
    f = w1.shape[1]
    return pl.pallas_call(
        _ffn_kernel,
        grid=(t // tm, f // tf),
        in_specs=[
            pl.BlockSpec((tm, d), lambda i, j: (i, 0)),
            pl.BlockSpec((d, tf), lambda i, j: (0, j)),
            pl.BlockSpec((tf, d), lambda i, j: (j, 0)),
            _const_spec((1, d)), _const_spec((1, d)),
        ],
        out_specs=pl.BlockSpec((tm, d), lambda i, j: (i, 0)),
        out_shape=jax.ShapeDtypeStruct((t, d), F32),
        scratch_shapes=[pltpu.VMEM((tm, d), BF16)],
        compiler_params=pltpu.CompilerParams(
            dimension_semantics=("parallel", "arbitrary"), vmem_limit_bytes=VMEM_LIMIT),
        name="ffn",
    )(x1, w1, w2, g2.reshape(1, d), b2.reshape(1, d))


def kernel(x, positions, ln_in_g, ln_in_b, w_in, g_cq, w_uq, g_ckv, w_uk, w_uv, conv_w, conv_b,
           g_conv_ln, b_conv_ln, w_out, g_ln1, b_ln1, w_ff1, w_ff2, g_ln2, b_ln2):
    b, s, d = x.shape
    w_in0 = w_in[0].astype(BF16)
    o_kr = Q_RANK + KV_RANK
    wc = w_in0[:, :o_kr]
    wkrT = w_in0[:, o_kr:o_kr + ROPE].T
    wconv = w_in0[:, o_kr + ROPE:]
    wuqT = w_uq[0].T.astype(BF16)
    wuk = w_uk[0].astype(BF16)
    wuvT = w_uv[0].T.astype(BF16)
    wa = w_out[0, :MLA_WIDTH].astype(BF16)
    wcv = w_out[0, MLA_WIDTH:].astype(BF16)
    w1 = w_ff1[0].astype(BF16)
    w2 = w_ff2[0].astype(BF16)

    cosT, sinT = _rope_tables(positions)
    qT, k, vT, u = _in_proj(x, ln_in_g, ln_in_b, wc, wkrT, wconv, g_cq[0], g_ckv[0],
                            wuqT, wuk, wuvT, cosT, sinT)
    y = _conv_module(u, conv_w[0], conv_b[0])
    attn = _attention(qT, k, vT)
    x2d = x.reshape(b * s, d)
    x1 = _merge(x2d, ln_in_g, ln_in_b, attn.reshape(b * s, MLA_WIDTH), y.reshape(b * s, SUB, LANE),
                g_conv_ln[0], b_conv_ln[0], wa, wcv, g_ln1[0], b_ln1[0])
    out = _ffn(x1, w1, w2, g_ln2[0], b_ln2[0])
    return out.reshape(b, s, d)
```

```python
import functools
import math

import jax
import jax.numpy as jnp
from jax import lax
from jax.experimental import pallas as pl
from jax.experimental.pallas import tpu as pltpu

F32 = jnp.float32
BF16 = jnp.bfloat16

D_MODEL = 2048
HEADS = 8
NOPE = 128
ROPE = 64
HALF = ROPE // 2
QK = NOPE + ROPE
QK_PAD = 256
V_DIM = 128
VT_ROWS = V_DIM + 16
Q_RANK = 512
KV_RANK = 512
MLA_WIDTH = HEADS * V_DIM
CONV_CH = D_MODEL - MLA_WIDTH
CONV_W = 31
CONV_PAD = CONV_W // 2
HALO = 16
D_FF = 4 * D_MODEL
ROPE_BASE = 10000.0
LN_EPS = 1e-5
RMS_EPS = 1e-6
ALPHA = 2.0 ** 0.25
Q_SCALE = (QK ** -0.5) * math.log2(math.e)

SUB = 8
LANE = 128
VMEM_LIMIT = 56 * 1024 * 1024

NT_DIMS = (((1,), (1,)), ((), ()))


def _const_spec(shape):
    nd = len(shape)
    return pl.BlockSpec(shape, lambda *_: (0,) * nd, pipeline_mode=pl.Buffered(1))


def _layer_norm(x, g, b):
    mu = jnp.mean(x, axis=-1, keepdims=True)
    xc = x - mu
    var = jnp.mean(xc * xc, axis=-1, keepdims=True)
    return xc * lax.rsqrt(var + LN_EPS) * g + b


def _rope_kernel(pos_ref, invf_ref, cos_ref, sin_ref):
    ang = invf_ref[...] * pos_ref[0].astype(F32)
    cos_ref[0] = jnp.cos(ang)
    sin_ref[0] = jnp.sin(ang)


def _rope_tables(positions, ts=2048):
    b, s = positions.shape
    inv_freq = ROPE_BASE ** (-jnp.arange(HALF, dtype=F32) * (2.0 / ROPE))
    out = jax.ShapeDtypeStruct((b, HALF, s), F32)
    return pl.pallas_call(
        _rope_kernel,
        grid=(b, s // ts),
        in_specs=[pl.BlockSpec((1, 1, ts), lambda i, j: (i, 0, j)),
                  pl.BlockSpec((HALF, 1), lambda i, j: (0, 0))],
        out_specs=[pl.BlockSpec((1, HALF, ts), lambda i, j: (i, 0, j))] * 2,
        out_shape=[out, out],
        name="rope_tables",
    )(positions.reshape(b, 1, s), inv_freq.reshape(HALF, 1))


def _inproj_kernel(x_ref, lng_ref, lnb_ref, wc_ref, wkrT_ref, wconv_ref, gcq_ref, gckv_ref,
                   wuqT_ref, wuk_ref, wuvT_ref, cos_ref, sin_ref,
                   qT_ref, k_ref, vT_ref, u_ref):
    tm = x_ref.shape[1]
    xb = _layer_norm(x_ref[0], lng_ref[...], lnb_ref[...]).astype(BF16)
    cos = cos_ref[0]
    sin = sin_ref[0]

    c = jnp.dot(xb, wc_ref[...], preferred_element_type=F32)
    c_q = c[:, :Q_RANK]
    c_kv = c[:, Q_RANK:]
    rq = lax.rsqrt(jnp.mean(c_q * c_q, axis=-1, keepdims=True) + RMS_EPS) * Q_SCALE
    cqn = (c_q * rq * gcq_ref[...]).astype(BF16)
    rkv = lax.rsqrt(jnp.mean(c_kv * c_kv, axis=-1, keepdims=True) + RMS_EPS)
    ckvn = (c_kv * rkv * gckv_ref[...]).astype(BF16)

    qT = lax.dot_general(wuqT_ref[...], cqn, NT_DIMS, preferred_element_type=F32)
    zeros_q = jnp.zeros((QK_PAD - QK, tm), BF16)
    for h in range(HEADS):
        base = h * QK
        r1 = qT[base + NOPE:base + NOPE + HALF]
        r2 = qT[base + NOPE + HALF:base + QK]
        qT_ref[0, h, 0:NOPE, :] = qT[base:base + NOPE].astype(BF16)
        qT_ref[0, h, NOPE:NOPE + HALF, :] = (r1 * cos - r2 * sin).astype(BF16)
        qT_ref[0, h, NOPE + HALF:QK, :] = (r2 * cos + r1 * sin).astype(BF16)
        qT_ref[0, h, QK:QK_PAD, :] = zeros_q

    krT = lax.dot_general(wkrT_ref[...], xb, NT_DIMS, preferred_element_type=F32)
    k1 = krT[:HALF]
    k2 = krT[HALF:]
    kr_fullT = jnp.concatenate(
        [k1 * cos - k2 * sin, k2 * cos + k1 * sin, jnp.zeros((QK_PAD - QK, tm), F32)], axis=0)
    kr = kr_fullT.T.astype(BF16)

    kn = jnp.dot(ckvn, wuk_ref[...], preferred_element_type=F32)
    vT = lax.dot_general(wuvT_ref[...], ckvn, NT_DIMS, preferred_element_type=F32)
    row = lax.broadcasted_iota(jnp.int32, (VT_ROWS - V_DIM, tm), 0)
    ones_row = jnp.where(row == 0, 1.0, 0.0).astype(BF16)
    for h in range(HEADS):
        k_ref[0, h, :, 0:NOPE] = kn[:, h * NOPE:(h + 1) * NOPE].astype(BF16)
        k_ref[0, h, :, NOPE:QK_PAD] = kr
        vT_ref[0, h, 0:V_DIM, :] = vT[h * V_DIM:(h + 1) * V_DIM].astype(BF16)
        vT_ref[0, h, V_DIM:VT_ROWS, :] = ones_row

    ag = jnp.dot(xb, wconv_ref[...], preferred_element_type=F32)
    u = ag[:, :CONV_CH] * jax.nn.sigmoid(ag[:, CONV_CH:])
    u_ref[0] = u.reshape(tm, SUB, LANE)


def _in_proj(x, ln_g, ln_b, wc, wkrT, wconv, g_cq, g_ckv, wuqT, wuk, wuvT, cosT, sinT, tm=512):
    b, s, d = x.shape
    grid = (b, s // tm)
    out_shape = [
        jax.ShapeDtypeStruct((b, HEADS, QK_PAD, s), BF16),
        jax.ShapeDtypeStruct((b, HEADS, s, QK_PAD), BF16),
        jax.ShapeDtypeStruct((b, HEADS, VT_ROWS, s), BF16),
        jax.ShapeDtypeStruct((b, s, SUB, LANE), F32),
    ]
    in_specs = [
        pl.BlockSpec((1, tm, d), lambda i, j: (i, j, 0)),
        _const_spec((1, d)), _const_spec((1, d)),
        _const_spec(wc.shape), _const_spec(wkrT.shape), _const_spec(wconv.shape),
        _const_spec((1, Q_RANK)), _const_spec((1, KV_RANK)),
        _const_spec(wuqT.shape), _const_spec(wuk.shape), _const_spec(wuvT.shape),
        pl.BlockSpec((1, HALF, tm), lambda i, j: (i, 0, j)),
        pl.BlockSpec((1, HALF, tm), lambda i, j: (i, 0, j)),
    ]
    out_specs = [
        pl.BlockSpec((1, HEADS, QK_PAD, tm), lambda i, j: (i, 0, 0, j)),
        pl.BlockSpec((1, HEADS, tm, QK_PAD), lambda i, j: (i, 0, j, 0)),
        pl.BlockSpec((1, HEADS, VT_ROWS, tm), lambda i, j: (i, 0, 0, j)),
        pl.BlockSpec((1, tm, SUB, LANE), lambda i, j: (i, j, 0, 0)),
    ]
    return pl.pallas_call(
        _inproj_kernel, grid=grid, in_specs=in_specs, out_specs=out_specs, out_shape=out_shape,
        compiler_params=pltpu.CompilerParams(
            dimension_semantics=("parallel", "parallel"), vmem_limit_bytes=VMEM_LIMIT),
        name="in_proj",
    )(x, ln_g.reshape(1, d), ln_b.reshape(1, d), wc, wkrT, wconv,
      g_cq.reshape(1, Q_RANK), g_ckv.reshape(1, KV_RANK), wuqT, wuk, wuvT, cosT, sinT)


CONV_TB = 16


def _conv_kernel(u_ref, prev_ref, next_ref, w_ref, b_ref, y_ref, ext_ref):
    tc = u_ref.shape[1]
    i = pl.program_id(1)
    last = pl.num_programs(1) - 1
    ext_ref[0:HALO] = jnp.where(i > 0, prev_ref[0], 0.0)
    ext_ref[HALO:HALO + tc] = u_ref[0]
    ext_ref[HALO + tc:HALO + tc + HALO] = jnp.where(i < last, next_ref[0], 0.0)
    off = HALO - CONV_PAD
    bias = b_ref[...]

    for t0 in range(0, tc, CONV_TB):
        acc = jnp.zeros((CONV_TB, SUB, LANE), F32)
        for k in range(CONV_W):
            acc = acc + w_ref[k] * ext_ref[t0 + off + k:t0 + off + k + CONV_TB]
        y_ref[0, t0:t0 + CONV_TB] = acc + bias


def _conv_module(u4, conv_w, conv_b, tc=512):
    b, s = u4.shape[:2]
    nh = tc // HALO
    n_halo_blocks = s // HALO
    return pl.pallas_call(
        _conv_kernel,
        grid=(b, s // tc),
        in_specs=[
            pl.BlockSpec((1, tc, SUB, LANE), lambda i, j: (i, j, 0, 0)),
            pl.BlockSpec((1, HALO, SUB, LANE), lambda i, j: (i, jnp.maximum(j * nh - 1, 0), 0, 0)),
            pl.BlockSpec((1, HALO, SUB, LANE),
                         lambda i, j: (i, jnp.minimum((j + 1) * nh, n_halo_blocks - 1), 0, 0)),
            _const_spec((CONV_W, SUB, LANE)), _const_spec((SUB, LANE)),
        ],
        out_specs=pl.BlockSpec((1, tc, SUB, LANE), lambda i, j: (i, j, 0, 0)),
        out_shape=jax.ShapeDtypeStruct((b, s, SUB, LANE), F32),
        scratch_shapes=[pltpu.VMEM((tc + 2 * HALO, SUB, LANE), F32)],
        compiler_params=pltpu.CompilerParams(dimension_semantics=("parallel", "parallel")),
        name="conv_module",
    )(u4, u4, u4, conv_w.reshape(CONV_W, SUB, LANE), conv_b.reshape(SUB, LANE))


def _attn_kernel(qT_ref, k_ref, vT_ref, o_ref, sa_ref, sb_ref, acc_ref, *, tq, tk, ts):
    s_len = k_ref.shape[2]
    nq = s_len // tq
    nchunk = s_len // tk
    neg = jnp.full((SUB, tq), -jnp.inf, F32)

    def q_block(q):
        return qT_ref[0, 0, :, pl.ds(pl.multiple_of(q * tq, tq), tq)]

    def score_chunk(start, qT, s_ref, mpart):
        s = jnp.dot(k_ref[0, 0, pl.ds(start, ts), :], qT, preferred_element_type=F32)
        s_ref[pl.ds(start, ts), :] = s
        return jnp.maximum(mpart, jnp.max(s.reshape(ts // SUB, SUB, tq), axis=0))

    def weight_chunk(start, s_ref, m):
        p = jnp.exp2(s_ref[pl.ds(start, ts), :] - m).astype(BF16)
        acc_ref[...] += jnp.dot(vT_ref[0, 0, :, pl.ds(start, ts)], p, preferred_element_type=F32)

    def finish(q):
        o = acc_ref[0:V_DIM, :] / acc_ref[V_DIM:V_DIM + 1, :]
        o_ref[0, pl.ds(pl.multiple_of(q * tq, tq), tq), :] = o.T.astype(o_ref.dtype)

    def sweep(q_next, s_next_ref, q_cur, s_cur_ref, mpart_cur):
        qT = None if q_next is None else q_block(q_next)
        if q_cur is not None:
            m = jnp.max(mpart_cur, axis=0, keepdims=True)
            acc_ref[...] = jnp.zeros(acc_ref.shape, F32)

        def body(c, mpart):
            for j in range(tk // ts):
                start = pl.multiple_of(c * tk + j * ts, ts)
                if q_cur is not None:
                    weight_chunk(start, s_cur_ref, m)
                if q_next is not None:
                    mpart = score_chunk(start, qT, s_next_ref, mpart)
            return mpart

        mpart_next = lax.fori_loop(0, nchunk, body, neg)
        if q_cur is not None:
            finish(q_cur)
        return mpart_next

    mpart_a = sweep(0, sa_ref, None, None, None)

    def pair(i, mpart_a):
        q = 2 * i
        mpart_b = sweep(q + 1, sb_ref, q, sa_ref, mpart_a)
        return sweep(q + 2, sa_ref, q + 1, sb_ref, mpart_b)

    mpart_a = lax.fori_loop(0, nq // 2 - 1, pair, mpart_a)
    mpart_b = sweep(nq - 1, sb_ref, nq - 2, sa_ref, mpart_a)
    sweep(None, None, nq - 1, sb_ref, mpart_b)


def _attention(qT, k, vT, tq=512, tk=8192, ts=256):
    b, h, _, s = qT.shape
    return pl.pallas_call(
        functools.partial(_attn_kernel, tq=tq, tk=tk, ts=ts),
        grid=(b, h),
        in_specs=[
            pl.BlockSpec((1, 1, QK_PAD, s), lambda bi, hi: (bi, hi, 0, 0), pipeline_mode=pl.Buffered(1)),
            pl.BlockSpec((1, 1, s, QK_PAD), lambda bi, hi: (bi, hi, 0, 0)),
            pl.BlockSpec((1, 1, VT_ROWS, s), lambda bi, hi: (bi, hi, 0, 0)),
        ],
        out_specs=pl.BlockSpec((1, s, V_DIM), lambda bi, hi: (bi, 0, hi)),
        out_shape=jax.ShapeDtypeStruct((b, s, h * V_DIM), BF16),
        scratch_shapes=[pltpu.VMEM((s, tq), F32), pltpu.VMEM((s, tq), F32), pltpu.VMEM((VT_ROWS, tq), F32)],
        compiler_params=pltpu.CompilerParams(
            dimension_semantics=("parallel", "parallel"), vmem_limit_bytes=VMEM_LIMIT),
        name="attention",
    )(qT, k, vT)


def _merge_kernel(x_ref, lng_ref, lnb_ref, a_ref, y_ref, gc_ref, bc_ref, wa_ref, wc_ref, g1_ref, b1_ref, o_ref):
    x0 = _layer_norm(x_ref[...], lng_ref[...], lnb_ref[...])
    yn = _layer_norm(y_ref[...].reshape(y_ref.shape[0], CONV_CH), gc_ref[...], bc_ref[...])
    conv_out = (yn * jax.nn.sigmoid(yn)).astype(BF16)
    mix = (jnp.dot(a_ref[...], wa_ref[...], preferred_element_type=F32)
           + jnp.dot(conv_out, wc_ref[...], preferred_element_type=F32))
    o_ref[...] = _layer_norm(ALPHA * x0 + mix, g1_ref[...], b1_ref[...])


def _merge(x2d, ln_g, ln_b, attn2d, y2d, gc, bc, wa, wc, g1, b1, tm=512):
    t, d = x2d.shape
    row = lambda i: (i, 0)
    return pl.pallas_call(
        _merge_kernel,
        grid=(t // tm,),
        in_specs=[
            pl.BlockSpec((tm, d), row), _const_spec((1, d)), _const_spec((1, d)),
            pl.BlockSpec((tm, MLA_WIDTH), row), pl.BlockSpec((tm, SUB, LANE), lambda i: (i, 0, 0)),
            _const_spec((1, CONV_CH)), _const_spec((1, CONV_CH)),
            _const_spec(wa.shape), _const_spec(wc.shape), _const_spec((1, d)), _const_spec((1, d)),
        ],
        out_specs=pl.BlockSpec((tm, d), row),
        out_shape=jax.ShapeDtypeStruct((t, d), F32),
        compiler_params=pltpu.CompilerParams(
            dimension_semantics=("parallel",), vmem_limit_bytes=VMEM_LIMIT),
        name="merge",
    )(x2d, ln_g.reshape(1, d), ln_b.reshape(1, d), attn2d, y2d, gc.reshape(1, CONV_CH), bc.reshape(1, CONV_CH),
      wa, wc, g1.reshape(1, d), b1.reshape(1, d))


def _ffn_kernel(x_ref, w1_ref, w2_ref, g_ref, b_ref, o_ref, xb_ref):
    j = pl.program_id(1)

    @pl.when(j == 0)
    def _():
        xb_ref[...] = x_ref[...].astype(BF16)
        o_ref[...] = jnp.zeros(o_ref.shape, F32)

    h = jnp.dot(xb_ref[...], w1_ref[...], preferred_element_type=F32)
    h = jnp.maximum(h, 0.0)
    o_ref[...] += jnp.dot((h * h).astype(BF16), w2_ref[...], preferred_element_type=F32)

    @pl.when(j == pl.num_programs(1) - 1)
    def _():
        o_ref[...] = _layer_norm(ALPHA * x_ref[...] + o_ref[...], g_ref[...], b_ref[...])


def _ffn(x1, w1, w2, g2, b2, tm=512, tf=2048):
    t, d = x1.shape
    f = w1.shape[1]
    return pl.pallas_call(
        _ffn_kernel,
        grid=(t // tm, f // tf),
        in_specs=[
            pl.BlockSpec((tm, d), lambda i, j: (i, 0)),
            pl.BlockSpec((d, tf), lambda i, j: (0, j)),
            pl.BlockSpec((tf, d), lambda i, j: (j, 0)),
            _const_spec((1, d)), _const_spec((1, d)),
        ],
        out_specs=pl.BlockSpec((tm, d), lambda i, j: (i, 0)),
        out_shape=jax.ShapeDtypeStruct((t, d), F32),
        scratch_shapes=[pltpu.VMEM((tm, d), BF16)],
        compiler_params=pltpu.CompilerParams(
            dimension_semantics=("parallel", "arbitrary"), vmem_limit_bytes=60 * 1024 * 1024),
        name="ffn",
    )(x1, w1, w2, g2.reshape(1, d), b2.reshape(1, d))


def kernel(x, positions, ln_in_g, ln_in_b, w_in, g_cq, w_uq, g_ckv, w_uk, w_uv, conv_w, conv_b,
           g_conv_ln, b_conv_ln, w_out, g_ln1, b_ln1, w_ff1, w_ff2, g_ln2, b_ln2):
    b, s, d = x.shape
    w_in0 = w_in[0].astype(BF16)
    o_kr = Q_RANK + KV_RANK
    wc = w_in0[:, :o_kr]
    wkrT = w_in0[:, o_kr:o_kr + ROPE].T
    wconv = w_in0[:, o_kr + ROPE:]
    wuqT = w_uq[0].T.astype(BF16)
    wuk = w_uk[0].astype(BF16)
    wuvT = w_uv[0].T.astype(BF16)
    wa = w_out[0, :MLA_WIDTH].astype(BF16)
    wcv = w_out[0, MLA_WIDTH:].astype(BF16)
    w1 = w_ff1[0].astype(BF16)
    w2 = w_ff2[0].astype(BF16)

    cosT, sinT = _rope_tables(positions)
    qT, k, vT, u = _in_proj(x, ln_in_g, ln_in_b, wc, wkrT, wconv, g_cq[0], g_ckv[0],
                            wuqT, wuk, wuvT, cosT, sinT)
    y = _conv_module(u, conv_w[0], conv_b[0])
    attn = _attention(qT, k, vT)
    x2d = x.reshape(b * s, d)
    x1 = _merge(x2d, ln_in_g, ln_in_b, attn.reshape(b * s, MLA_WIDTH), y.reshape(b * s, SUB, LANE),
                g_conv_ln[0], b_conv_ln[0], wa, wcv, g_ln1[0], b_ln1[0])
    out = _ffn(x1, w1, w2, g_ln2[0], b_ln2[0])
    return out.reshape(b, s, d)
```

```python
import functools
import math

import jax
import jax.numpy as jnp
from jax import lax
from jax.experimental import pallas as pl
from jax.experimental.pallas import tpu as pltpu

F32 = jnp.float32
BF16 = jnp.bfloat16

D_MODEL = 2048
HEADS = 8
NOPE = 128
ROPE = 64
HALF = ROPE // 2
QK = NOPE + ROPE
QK_PAD = 256
V_DIM = 128
VT_ROWS = V_DIM + 16
Q_RANK = 512
KV_RANK = 512
MLA_WIDTH = HEADS * V_DIM
CONV_CH = D_MODEL - MLA_WIDTH
CONV_W = 31
CONV_PAD = CONV_W // 2
HALO = 16
D_FF = 4 * D_MODEL
ROPE_BASE = 10000.0
LN_EPS = 1e-5
RMS_EPS = 1e-6
ALPHA = 2.0 ** 0.25
Q_SCALE = (QK ** -0.5) * math.log2(math.e)

SUB = 8
LANE = 128
VMEM_LIMIT = 56 * 1024 * 1024
VMEM_LIMIT_BIG = 60 * 1024 * 1024

NT_DIMS = (((1,), (1,)), ((), ()))


def _const_spec(shape):
    nd = len(shape)
    return pl.BlockSpec(shape, lambda *_: (0,) * nd, pipeline_mode=pl.Buffered(1))


def _layer_norm(x, g, b):
    mu = jnp.mean(x, axis=-1, keepdims=True)
    xc = x - mu
    var = jnp.mean(xc * xc, axis=-1, keepdims=True)
    return xc * lax.rsqrt(var + LN_EPS) * g + b


def _rope_kernel(pos_ref, invf_ref, cos_ref, sin_ref):
    ang = invf_ref[...] * pos_ref[0].astype(F32)
    cos_ref[0] = jnp.cos(ang)
    sin_ref[0] = jnp.sin(ang)


def _rope_tables(positions, ts=2048):
    b, s = positions.shape
    inv_freq = ROPE_BASE ** (-jnp.arange(HALF, dtype=F32) * (2.0 / ROPE))
    out = jax.ShapeDtypeStruct((b, HALF, s), F32)
    return pl.pallas_call(
        _rope_kernel,
        grid=(b, s // ts),
        in_specs=[pl.BlockSpec((1, 1, ts), lambda i, j: (i, 0, j)),
                  pl.BlockSpec((HALF, 1), lambda i, j: (0, 0))],
        out_specs=[pl.BlockSpec((1, HALF, ts), lambda i, j: (i, 0, j))] * 2,
        out_shape=[out, out],
        name="rope_tables",
    )(positions.reshape(b, 1, s), inv_freq.reshape(HALF, 1))


def _inproj_kernel(x_ref, lng_ref, lnb_ref, wc_ref, wkrT_ref, wconv_ref, gcq_ref, gckv_ref,
                   wuqT_ref, wuk_ref, wuvT_ref, cos_ref, sin_ref,
                   qT_ref, k_ref, vT_ref, u_ref):
    tm = x_ref.shape[1]
    xb = _layer_norm(x_ref[0], lng_ref[...], lnb_ref[...]).astype(BF16)
    cos = cos_ref[0]
    sin = sin_ref[0]

    c = jnp.dot(xb, wc_ref[...], preferred_element_type=F32)
    c_q = c[:, :Q_RANK]
    c_kv = c[:, Q_RANK:]
    rq = lax.rsqrt(jnp.mean(c_q * c_q, axis=-1, keepdims=True) + RMS_EPS) * Q_SCALE
    cqn = (c_q * rq * gcq_ref[...]).astype(BF16)
    rkv = lax.rsqrt(jnp.mean(c_kv * c_kv, axis=-1, keepdims=True) + RMS_EPS)
    ckvn = (c_kv * rkv * gckv_ref[...]).astype(BF16)

    qT = lax.dot_general(wuqT_ref[...], cqn, NT_DIMS, preferred_element_type=F32)
    zeros_q = jnp.zeros((QK_PAD - QK, tm), BF16)
    for h in range(HEADS):
        base = h * QK
        r1 = qT[base + NOPE:base + NOPE + HALF]
        r2 = qT[base + NOPE + HALF:base + QK]
        qT_ref[0, h, 0:NOPE, :] = qT[base:base + NOPE].astype(BF16)
        qT_ref[0, h, NOPE:NOPE + HALF, :] = (r1 * cos - r2 * sin).astype(BF16)
        qT_ref[0, h, NOPE + HALF:QK, :] = (r2 * cos + r1 * sin).astype(BF16)
        qT_ref[0, h, QK:QK_PAD, :] = zeros_q

    krT = lax.dot_general(wkrT_ref[...], xb, NT_DIMS, preferred_element_type=F32)
    k1 = krT[:HALF]
    k2 = krT[HALF:]
    kr_fullT = jnp.concatenate(
        [k1 * cos - k2 * sin, k2 * cos + k1 * sin, jnp.zeros((QK_PAD - QK, tm), F32)], axis=0)
    kr = kr_fullT.T.astype(BF16)

    kn = jnp.dot(ckvn, wuk_ref[...], preferred_element_type=F32)
    vT = lax.dot_general(wuvT_ref[...], ckvn, NT_DIMS, preferred_element_type=F32)
    row = lax.broadcasted_iota(jnp.int32, (VT_ROWS - V_DIM, tm), 0)
    ones_row = jnp.where(row == 0, 1.0, 0.0).astype(BF16)
    for h in range(HEADS):
        k_ref[0, h, :, 0:NOPE] = kn[:, h * NOPE:(h + 1) * NOPE].astype(BF16)
        k_ref[0, h, :, NOPE:QK_PAD] = kr
        vT_ref[0, h, 0:V_DIM, :] = vT[h * V_DIM:(h + 1) * V_DIM].astype(BF16)
        vT_ref[0, h, V_DIM:VT_ROWS, :] = ones_row

    ag = jnp.dot(xb, wconv_ref[...], preferred_element_type=F32)
    u = ag[:, :CONV_CH] * jax.nn.sigmoid(ag[:, CONV_CH:])
    u_ref[0] = u.reshape(tm, SUB, LANE)


def _in_proj(x, ln_g, ln_b, wc, wkrT, wconv, g_cq, g_ckv, wuqT, wuk, wuvT, cosT, sinT, tm=512):
    b, s, d = x.shape
    grid = (b, s // tm)
    out_shape = [
        jax.ShapeDtypeStruct((b, HEADS, QK_PAD, s), BF16),
        jax.ShapeDtypeStruct((b, HEADS, s, QK_PAD), BF16),
        jax.ShapeDtypeStruct((b, HEADS, VT_ROWS, s), BF16),
        jax.ShapeDtypeStruct((b, s, SUB, LANE), F32),
    ]
    in_specs = [
        pl.BlockSpec((1, tm, d), lambda i, j: (i, j, 0)),
        _const_spec((1, d)), _const_spec((1, d)),
        _const_spec(wc.shape), _const_spec(wkrT.shape), _const_spec(wconv.shape),
        _const_spec((1, Q_RANK)), _const_spec((1, KV_RANK)),
        _const_spec(wuqT.shape), _const_spec(wuk.shape), _const_spec(wuvT.shape),
        pl.BlockSpec((1, HALF, tm), lambda i, j: (i, 0, j)),
        pl.BlockSpec((1, HALF, tm), lambda i, j: (i, 0, j)),
    ]
    out_specs = [
        pl.BlockSpec((1, HEADS, QK_PAD, tm), lambda i, j: (i, 0, 0, j)),
        pl.BlockSpec((1, HEADS, tm, QK_PAD), lambda i, j: (i, 0, j, 0)),
        pl.BlockSpec((1, HEADS, VT_ROWS, tm), lambda i, j: (i, 0, 0, j)),
        pl.BlockSpec((1, tm, SUB, LANE), lambda i, j: (i, j, 0, 0)),
    ]
    return pl.pallas_call(
        _inproj_kernel, grid=grid, in_specs=in_specs, out_specs=out_specs, out_shape=out_shape,
        compiler_params=pltpu.CompilerParams(
            dimension_semantics=("parallel", "parallel"), vmem_limit_bytes=VMEM_LIMIT),
        name="in_proj",
    )(x, ln_g.reshape(1, d), ln_b.reshape(1, d), wc, wkrT, wconv,
      g_cq.reshape(1, Q_RANK), g_ckv.reshape(1, KV_RANK), wuqT, wuk, wuvT, cosT, sinT)


CONV_TB = 16


def _conv_kernel(u_ref, prev_ref, next_ref, w_ref, b_ref, y_ref, ext_ref):
    tc = u_ref.shape[1]
    i = pl.program_id(1)
    last = pl.num_programs(1) - 1
    ext_ref[0:HALO] = jnp.where(i > 0, prev_ref[0], 0.0)
    ext_ref[HALO:HALO + tc] = u_ref[0]
    ext_ref[HALO + tc:HALO + tc + HALO] = jnp.where(i < last, next_ref[0], 0.0)
    off = HALO - CONV_PAD
    bias = b_ref[...]

    for t0 in range(0, tc, CONV_TB):
        acc = jnp.zeros((CONV_TB, SUB, LANE), F32)
        for k in range(CONV_W):
            acc = acc + w_ref[k] * ext_ref[t0 + off + k:t0 + off + k + CONV_TB]
        y_ref[0, t0:t0 + CONV_TB] = acc + bias


def _conv_module(u4, conv_w, conv_b, tc=512):
    b, s = u4.shape[:2]
    nh = tc // HALO
    n_halo_blocks = s // HALO
    return pl.pallas_call(
        _conv_kernel,
        grid=(b, s // tc),
        in_specs=[
            pl.BlockSpec((1, tc, SUB, LANE), lambda i, j: (i, j, 0, 0)),
            pl.BlockSpec((1, HALO, SUB, LANE), lambda i, j: (i, jnp.maximum(j * nh - 1, 0), 0, 0)),
            pl.BlockSpec((1, HALO, SUB, LANE),
                         lambda i, j: (i, jnp.minimum((j + 1) * nh, n_halo_blocks - 1), 0, 0)),
            _const_spec((CONV_W, SUB, LANE)), _const_spec((SUB, LANE)),
        ],
        out_specs=pl.BlockSpec((1, tc, SUB, LANE), lambda i, j: (i, j, 0, 0)),
        out_shape=jax.ShapeDtypeStruct((b, s, SUB, LANE), F32),
        scratch_shapes=[pltpu.VMEM((tc + 2 * HALO, SUB, LANE), F32)],
        compiler_params=pltpu.CompilerParams(dimension_semantics=("parallel", "parallel")),
        name="conv_module",
    )(u4, u4, u4, conv_w.reshape(CONV_W, SUB, LANE), conv_b.reshape(SUB, LANE))


def _attn_kernel(qT_ref, k_ref, vT_ref, o_ref, sa_ref, sb_ref, acc_ref, *, tq, tk, ts):
    s_len = k_ref.shape[2]
    nq = s_len // tq
    nchunk = s_len // tk
    neg = jnp.full((SUB, tq), -jnp.inf, F32)

    def q_block(q):
        return qT_ref[0, 0, :, pl.ds(pl.multiple_of(q * tq, tq), tq)]

    def score_chunk(start, qT, s_ref, mpart):
        s = jnp.dot(k_ref[0, 0, pl.ds(start, ts), :], qT, preferred_element_type=F32)
        s_ref[pl.ds(start, ts), :] = s
        return jnp.maximum(mpart, jnp.max(s.reshape(ts // SUB, SUB, tq), axis=0))

    def weight_chunk(start, s_ref, m):
        p = jnp.exp2(s_ref[pl.ds(start, ts), :] - m).astype(BF16)
        acc_ref[...] += jnp.dot(vT_ref[0, 0, :, pl.ds(start, ts)], p, preferred_element_type=F32)

    def finish(q):
        o = acc_ref[0:V_DIM, :] / acc_ref[V_DIM:V_DIM + 1, :]
        o_ref[0, pl.ds(pl.multiple_of(q * tq, tq), tq), :] = o.T.astype(o_ref.dtype)

    def sweep(q_next, s_next_ref, q_cur, s_cur_ref, mpart_cur):
        qT = None if q_next is None else q_block(q_next)
        if q_cur is not None:
            m = jnp.max(mpart_cur, axis=0, keepdims=True)
            acc_ref[...] = jnp.zeros(acc_ref.shape, F32)

        def body(c, mpart):
            for j in range(tk // ts):
                start = pl.multiple_of(c * tk + j * ts, ts)
                if q_cur is not None:
                    weight_chunk(start, s_cur_ref, m)
                if q_next is not None:
                    mpart = score_chunk(start, qT, s_next_ref, mpart)
            return mpart

        mpart_next = lax.fori_loop(0, nchunk, body, neg)
        if q_cur is not None:
            finish(q_cur)
        return mpart_next

    mpart_a = sweep(0, sa_ref, None, None, None)

    def pair(i, mpart_a):
        q = 2 * i
        mpart_b = sweep(q + 1, sb_ref, q, sa_ref, mpart_a)
        return sweep(q + 2, sa_ref, q + 1, sb_ref, mpart_b)

    mpart_a = lax.fori_loop(0, nq // 2 - 1, pair, mpart_a)
    mpart_b = sweep(nq - 1, sb_ref, nq - 2, sa_ref, mpart_a)
    sweep(None, None, nq - 1, sb_ref, mpart_b)


def _attention(qT, k, vT, tq=512, tk=8192, ts=256):
    b, h, _, s = qT.shape
    return pl.pallas_call(
        functools.partial(_attn_kernel, tq=tq, tk=tk, ts=ts),
        grid=(b, h),
        in_specs=[
            pl.BlockSpec((1, 1, QK_PAD, s), lambda bi, hi: (bi, hi, 0, 0)),
            pl.BlockSpec((1, 1, s, QK_PAD), lambda bi, hi: (bi, hi, 0, 0)),
            pl.BlockSpec((1, 1, VT_ROWS, s), lambda bi, hi: (bi, hi, 0, 0)),
        ],
        out_specs=pl.BlockSpec((1, s, V_DIM), lambda bi, hi: (bi, 0, hi)),
        out_shape=jax.ShapeDtypeStruct((b, s, h * V_DIM), BF16),
        scratch_shapes=[pltpu.VMEM((s, tq), F32), pltpu.VMEM((s, tq), F32), pltpu.VMEM((VT_ROWS, tq), F32)],
        compiler_params=pltpu.CompilerParams(
            dimension_semantics=("parallel", "parallel"), vmem_limit_bytes=VMEM_LIMIT_BIG),
        name="attention",
    )(qT, k, vT)


def _merge_kernel(x_ref, lng_ref, lnb_ref, a_ref, y_ref, gc_ref, bc_ref, wa_ref, wc_ref, g1_ref, b1_ref, o_ref):
    x0 = _layer_norm(x_ref[...], lng_ref[...], lnb_ref[...])
    yn = _layer_norm(y_ref[...].reshape(y_ref.shape[0], CONV_CH), gc_ref[...], bc_ref[...])
    conv_out = (yn * jax.nn.sigmoid(yn)).astype(BF16)
    mix = (jnp.dot(a_ref[...], wa_ref[...], preferred_element_type=F32)
           + jnp.dot(conv_out, wc_ref[...], preferred_element_type=F32))
    o_ref[...] = _layer_norm(ALPHA * x0 + mix, g1_ref[...], b1_ref[...])


def _merge(x2d, ln_g, ln_b, attn2d, y2d, gc, bc, wa, wc, g1, b1, tm=512):
    t, d = x2d.shape
    row = lambda i: (i, 0)
    return pl.pallas_call(
        _merge_kernel,
        grid=(t // tm,),
        in_specs=[
            pl.BlockSpec((tm, d), row), _const_spec((1, d)), _const_spec((1, d)),
            pl.BlockSpec((tm, MLA_WIDTH), row), pl.BlockSpec((tm, SUB, LANE), lambda i: (i, 0, 0)),
            _const_spec((1, CONV_CH)), _const_spec((1, CONV_CH)),
            _const_spec(wa.shape), _const_spec(wc.shape), _const_spec((1, d)), _const_spec((1, d)),
        ],
        out_specs=pl.BlockSpec((tm, d), row),
        out_shape=jax.ShapeDtypeStruct((t, d), F32),
        compiler_params=pltpu.CompilerParams(
            dimension_semantics=("parallel",), vmem_limit_bytes=VMEM_LIMIT),
        name="merge",
    )(x2d, ln_g.reshape(1, d), ln_b.reshape(1, d), attn2d, y2d, gc.reshape(1, CONV_CH), bc.reshape(1, CONV_CH),
      wa, wc, g1.reshape(1, d), b1.reshape(1, d))


def _ffn_kernel(x_ref, w1_ref, w2_ref, g_ref, b_ref, o_ref, xb_ref):
    j = pl.program_id(1)

    @pl.when(j == 0)
    def _():
        xb_ref[...] = x_ref[...].astype(BF16)
        o_ref[...] = jnp.zeros(o_ref.shape, F32)

    h = jnp.dot(xb_ref[...], w1_ref[...], preferred_element_type=F32)
    h = jnp.maximum(h, 0.0)
    o_ref[...] += jnp.dot((h * h).astype(BF16), w2_ref[...], preferred_element_type=F32)

    @pl.when(j == pl.num_programs(1) - 1)
    def _():
        o_ref[...] = _layer_norm(ALPHA * x_ref[...] + o_ref[...], g_ref[...], b_ref[...])


def _ffn(x1, w1, w2, g2, b2, tm=512, tf=2048):
    t, d = x1.shape
    f = w1.shape[1]
    return pl.pallas_call(
        _ffn_kernel,
        grid=(t // tm, f // tf),
        in_specs=[
            pl.BlockSpec((tm, d), lambda i, j: (i, 0)),
            pl.BlockSpec((d, tf), lambda i, j: (0, j)),
            pl.BlockSpec((tf, d), lambda i, j: (j, 0)),
            _const_spec((1, d)), _const_spec((1, d)),
        ],
        out_specs=pl.BlockSpec((tm, d), lambda i, j: (i, 0)),
        out_shape=jax.ShapeDtypeStruct((t, d), F32),
        scratch_shapes=[pltpu.VMEM((tm, d), BF16)],
        compiler_params=pltpu.CompilerParams(
            dimension_semantics=("parallel", "arbitrary"), vmem_limit_bytes=VMEM_LIMIT_BIG),
        name="ffn",
    )(x1, w1, w2, g2.reshape(1, d), b2.reshape(1, d))


def kernel(x, positions, ln_in_g, ln_in_b, w_in, g_cq, w_uq, g_ckv, w_uk, w_uv, conv_w, conv_b,
           g_conv_ln, b_conv_ln, w_out, g_ln1, b_ln1, w_ff1, w_ff2, g_ln2, b_ln2):
    b, s, d = x.shape
    w_in0 = w_in[0].astype(BF16)
    o_kr = Q_RANK + KV_RANK
    wc = w_in0[:, :o_kr]
    wkrT = w_in0[:, o_kr:o_kr + ROPE].T
    wconv = w_in0[:, o_kr + ROPE:]
    wuqT = w_uq[0].T.astype(BF16)
    wuk = w_uk[0].astype(BF16)
    wuvT = w_uv[0].T.astype(BF16)
    wa = w_out[0, :MLA_WIDTH].astype(BF16)
    wcv = w_out[0, MLA_WIDTH:].astype(BF16)
    w1 = w_ff1[0].astype(BF16)
    w2 = w_ff2[0].astype(BF16)

    cosT, sinT = _rope_tables(positions)
    qT, k, vT, u = _in_proj(x, ln_in_g, ln_in_b, wc, wkrT, wconv, g_cq[0], g_ckv[0],
                            wuqT, wuk, wuvT, cosT, sinT)
    y = _conv_module(u, conv_w[0], conv_b[0])
    attn = _attention(qT, k, vT)
    x2d = x.reshape(b * s, d)
    x1 = _merge(x2d, ln_in_g, ln_in_b, attn.reshape(b * s, MLA_WIDTH), y.reshape(b * s, SUB, LANE),
                g_conv_ln[0], b_conv_ln[0], wa, wcv, g_ln1[0], b_ln1[0])
    out = _ffn(x1, w1, w2, g_ln2[0], b_ln2[0])
    return out.reshape(b, s, d)
```

```python
import functools
import math

import jax
import jax.numpy as jnp
from jax import lax
from jax.experimental import pallas as pl
from jax.experimental.pallas import tpu as pltpu

F32 = jnp.float32
BF16 = jnp.bfloat16

D_MODEL = 2048
HEADS = 8
NOPE = 128
ROPE = 64
HALF = ROPE // 2
QK = NOPE + ROPE
QK_PAD = 256
V_DIM = 128
VT_ROWS = V_DIM + 16
Q_RANK = 512
KV_RANK = 512
MLA_WIDTH = HEADS * V_DIM
CONV_CH = D_MODEL - MLA_WIDTH
CONV_W = 31
CONV_PAD = CONV_W // 2
HALO = 16
D_FF = 4 * D_MODEL
ROPE_BASE = 10000.0
LN_EPS = 1e-5
RMS_EPS = 1e-6
ALPHA = 2.0 ** 0.25
Q_SCALE = (QK ** -0.5) * math.log2(math.e)

SUB = 8
LANE = 128
VMEM_LIMIT = 56 * 1024 * 1024
VMEM_LIMIT_BIG = 60 * 1024 * 1024

NT_DIMS = (((1,), (1,)), ((), ()))


def _const_spec(shape):
    nd = len(shape)
    return pl.BlockSpec(shape, lambda *_: (0,) * nd, pipeline_mode=pl.Buffered(1))


def _layer_norm(x, g, b):
    mu = jnp.mean(x, axis=-1, keepdims=True)
    xc = x - mu
    var = jnp.mean(xc * xc, axis=-1, keepdims=True)
    return xc * lax.rsqrt(var + LN_EPS) * g + b


def _rope_kernel(pos_ref, invf_ref, cos_ref, sin_ref):
    ang = invf_ref[...] * pos_ref[0].astype(F32)
    cos_ref[0] = jnp.cos(ang)
    sin_ref[0] = jnp.sin(ang)


def _rope_tables(positions, ts=2048):
    b, s = positions.shape
    assert s % ts == 0
    inv_freq = ROPE_BASE ** (-jnp.arange(HALF, dtype=F32) * (2.0 / ROPE))
    out = jax.ShapeDtypeStruct((b, HALF, s), F32)
    return pl.pallas_call(
        _rope_kernel,
        grid=(b, s // ts),
        in_specs=[pl.BlockSpec((1, 1, ts), lambda i, j: (i, 0, j)),
                  pl.BlockSpec((HALF, 1), lambda i, j: (0, 0))],
        out_specs=[pl.BlockSpec((1, HALF, ts), lambda i, j: (i, 0, j))] * 2,
        out_shape=[out, out],
        name="rope_tables",
    )(positions.reshape(b, 1, s), inv_freq.reshape(HALF, 1))


def _inproj_kernel(x_ref, lng_ref, lnb_ref, wc_ref, wkrT_ref, wconv_ref, gcq_ref, gckv_ref,
                   wuqT_ref, wuk_ref, wuvT_ref, cos_ref, sin_ref,
                   qT_ref, k_ref, vT_ref, u_ref):
    tm = x_ref.shape[1]
    xb = _layer_norm(x_ref[0], lng_ref[...], lnb_ref[...]).astype(BF16)
    cos = cos_ref[0]
    sin = sin_ref[0]

    ag = jnp.dot(xb, wconv_ref[...], preferred_element_type=F32)
    u = ag[:, :CONV_CH] * jax.nn.sigmoid(ag[:, CONV_CH:])
    u_ref[0] = u.reshape(tm, SUB, LANE)

    c = jnp.dot(xb, wc_ref[...], preferred_element_type=F32)
    c_q = c[:, :Q_RANK]
    c_kv = c[:, Q_RANK:]
    rq = lax.rsqrt(jnp.mean(c_q * c_q, axis=-1, keepdims=True) + RMS_EPS) * Q_SCALE
    cqn = (c_q * rq * gcq_ref[...]).astype(BF16)
    rkv = lax.rsqrt(jnp.mean(c_kv * c_kv, axis=-1, keepdims=True) + RMS_EPS)
    ckvn = (c_kv * rkv * gckv_ref[...]).astype(BF16)

    qT = lax.dot_general(wuqT_ref[...], cqn, NT_DIMS, preferred_element_type=F32)
    zeros_q = jnp.zeros((QK_PAD - QK, tm), BF16)
    for h in range(HEADS):
        base = h * QK
        r1 = qT[base + NOPE:base + NOPE + HALF]
        r2 = qT[base + NOPE + HALF:base + QK]
        qT_ref[0, h, 0:NOPE, :] = qT[base:base + NOPE].astype(BF16)
        qT_ref[0, h, NOPE:NOPE + HALF, :] = (r1 * cos - r2 * sin).astype(BF16)
        qT_ref[0, h, NOPE + HALF:QK, :] = (r2 * cos + r1 * sin).astype(BF16)
        qT_ref[0, h, QK:QK_PAD, :] = zeros_q

    krT = lax.dot_general(wkrT_ref[...], xb, NT_DIMS, preferred_element_type=F32)
    k1 = krT[:HALF]
    k2 = krT[HALF:]
    kr_fullT = jnp.concatenate(
        [k1 * cos - k2 * sin, k2 * cos + k1 * sin, jnp.zeros((QK_PAD - QK, tm), F32)], axis=0)
    kr = kr_fullT.T.astype(BF16)

    kn = jnp.dot(ckvn, wuk_ref[...], preferred_element_type=F32)
    vT = lax.dot_general(wuvT_ref[...], ckvn, NT_DIMS, preferred_element_type=F32)
    row = lax.broadcasted_iota(jnp.int32, (VT_ROWS - V_DIM, tm), 0)
    ones_row = jnp.where(row == 0, 1.0, 0.0).astype(BF16)
    for h in range(HEADS):
        k_ref[0, h, :, 0:NOPE] = kn[:, h * NOPE:(h + 1) * NOPE].astype(BF16)
        k_ref[0, h, :, NOPE:QK_PAD] = kr
        vT_ref[0, h, 0:V_DIM, :] = vT[h * V_DIM:(h + 1) * V_DIM].astype(BF16)
        vT_ref[0, h, V_DIM:VT_ROWS, :] = ones_row


def _in_proj(x, ln_g, ln_b, wc, wkrT, wconv, g_cq, g_ckv, wuqT, wuk, wuvT, cosT, sinT, tm=512):
    b, s, d = x.shape
    assert s % tm == 0 and d == D_MODEL and wconv.shape == (d, 2 * CONV_CH)
    grid = (b, s // tm)
    out_shape = [
        jax.ShapeDtypeStruct((b, HEADS, QK_PAD, s), BF16),
        jax.ShapeDtypeStruct((b, HEADS, s, QK_PAD), BF16),
        jax.ShapeDtypeStruct((b, HEADS, VT_ROWS, s), BF16),
        jax.ShapeDtypeStruct((b, s, SUB, LANE), F32),
    ]
    in_specs = [
        pl.BlockSpec((1, tm, d), lambda i, j: (i, j, 0)),
        _const_spec((1, d)), _const_spec((1, d)),
        _const_spec(wc.shape), _const_spec(wkrT.shape), _const_spec(wconv.shape),
        _const_spec((1, Q_RANK)), _const_spec((1, KV_RANK)),
        _const_spec(wuqT.shape), _const_spec(wuk.shape), _const_spec(wuvT.shape),
        pl.BlockSpec((1, HALF, tm), lambda i, j: (i, 0, j)),
        pl.BlockSpec((1, HALF, tm), lambda i, j: (i, 0, j)),
    ]
    out_specs = [
        pl.BlockSpec((1, HEADS, QK_PAD, tm), lambda i, j: (i, 0, 0, j)),
        pl.BlockSpec((1, HEADS, tm, QK_PAD), lambda i, j: (i, 0, j, 0)),
        pl.BlockSpec((1, HEADS, VT_ROWS, tm), lambda i, j: (i, 0, 0, j)),
        pl.BlockSpec((1, tm, SUB, LANE), lambda i, j: (i, j, 0, 0)),
    ]
    return pl.pallas_call(
        _inproj_kernel, grid=grid, in_specs=in_specs, out_specs=out_specs, out_shape=out_shape,
        compiler_params=pltpu.CompilerParams(
            dimension_semantics=("parallel", "parallel"), vmem_limit_bytes=VMEM_LIMIT),
        name="in_proj",
    )(x, ln_g.reshape(1, d), ln_b.reshape(1, d), wc, wkrT, wconv,
      g_cq.reshape(1, Q_RANK), g_ckv.reshape(1, KV_RANK), wuqT, wuk, wuvT, cosT, sinT)


CONV_TB = 16


def _conv_kernel(u_ref, prev_ref, next_ref, w_ref, b_ref, y_ref, ext_ref):
    tc = u_ref.shape[1]
    i = pl.program_id(1)
    last = pl.num_programs(1) - 1
    ext_ref[0:HALO] = jnp.where(i > 0, prev_ref[0], 0.0)
    ext_ref[HALO:HALO + tc] = u_ref[0]
    ext_ref[HALO + tc:HALO + tc + HALO] = jnp.where(i < last, next_ref[0], 0.0)
    off = HALO - CONV_PAD
    bias = b_ref[...]

    for t0 in range(0, tc, CONV_TB):
        acc = jnp.zeros((CONV_TB, SUB, LANE), F32)
        for k in range(CONV_W):
            acc = acc + w_ref[k] * ext_ref[t0 + off + k:t0 + off + k + CONV_TB]
        y_ref[0, t0:t0 + CONV_TB] = acc + bias


def _conv_module(u4, conv_w, conv_b, tc=512):
    b, s = u4.shape[:2]
    assert s % tc == 0 and tc % CONV_TB == 0 and tc % HALO == 0 and HALO >= CONV_PAD
    nh = tc // HALO
    n_halo_blocks = s // HALO
    return pl.pallas_call(
        _conv_kernel,
        grid=(b, s // tc),
        in_specs=[
            pl.BlockSpec((1, tc, SUB, LANE), lambda i, j: (i, j, 0, 0)),
            pl.BlockSpec((1, HALO, SUB, LANE), lambda i, j: (i, jnp.maximum(j * nh - 1, 0), 0, 0)),
            pl.BlockSpec((1, HALO, SUB, LANE),
                         lambda i, j: (i, jnp.minimum((j + 1) * nh, n_halo_blocks - 1), 0, 0)),
            _const_spec((CONV_W, SUB, LANE)), _const_spec((SUB, LANE)),
        ],
        out_specs=pl.BlockSpec((1, tc, SUB, LANE), lambda i, j: (i, j, 0, 0)),
        out_shape=jax.ShapeDtypeStruct((b, s, SUB, LANE), F32),
        scratch_shapes=[pltpu.VMEM((tc + 2 * HALO, SUB, LANE), F32)],
        compiler_params=pltpu.CompilerParams(dimension_semantics=("parallel", "parallel")),
        name="conv_module",
    )(u4, u4, u4, conv_w.reshape(CONV_W, SUB, LANE), conv_b.reshape(SUB, LANE))


def _attn_kernel(qT_ref, k_ref, vT_ref, o_ref, sa_ref, sb_ref, acc_ref, *, tq, tk, ts):
    s_len = k_ref.shape[2]
    nq = s_len // tq
    nchunk = s_len // tk
    neg = jnp.full((SUB, tq), -jnp.inf, F32)

    def q_block(q):
        return qT_ref[0, 0, :, pl.ds(pl.multiple_of(q * tq, tq), tq)]

    def score_chunk(start, qT, s_ref, mpart):
        s = jnp.dot(k_ref[0, 0, pl.ds(start, ts), :], qT, preferred_element_type=F32)
        s_ref[pl.ds(start, ts), :] = s
        return jnp.maximum(mpart, jnp.max(s.reshape(ts // SUB, SUB, tq), axis=0))

    def weight_chunk(start, s_ref, m):
        p = jnp.exp2(s_ref[pl.ds(start, ts), :] - m).astype(BF16)
        acc_ref[...] += jnp.dot(vT_ref[0, 0, :, pl.ds(start, ts)], p, preferred_element_type=F32)

    def finish(q):
        o = acc_ref[0:V_DIM, :] / acc_ref[V_DIM:V_DIM + 1, :]
        o_ref[0, pl.ds(pl.multiple_of(q * tq, tq), tq), :] = o.T.astype(o_ref.dtype)

    def sweep(q_next, s_next_ref, q_cur, s_cur_ref, mpart_cur):
        qT = None if q_next is None else q_block(q_next)
        if q_cur is not None:
            m = jnp.max(mpart_cur, axis=0, keepdims=True)
            acc_ref[...] = jnp.zeros(acc_ref.shape, F32)

        def body(c, mpart):
            for j in range(tk // ts):
                start = pl.multiple_of(c * tk + j * ts, ts)
                if q_cur is not None:
                    weight_chunk(start, s_cur_ref, m)
                if q_next is not None:
                    mpart = score_chunk(start, qT, s_next_ref, mpart)
            return mpart

        mpart_next = lax.fori_loop(0, nchunk, body, neg)
        if q_cur is not None:
            finish(q_cur)
        return mpart_next

    mpart_a = sweep(0, sa_ref, None, None, None)

    def pair(i, mpart_a):
        q = 2 * i
        mpart_b = sweep(q + 1, sb_ref, q, sa_ref, mpart_a)
        return sweep(q + 2, sa_ref, q + 1, sb_ref, mpart_b)

    mpart_a = lax.fori_loop(0, nq // 2 - 1, pair, mpart_a)
    mpart_b = sweep(nq - 1, sb_ref, nq - 2, sa_ref, mpart_a)
    sweep(None, None, nq - 1, sb_ref, mpart_b)


def _attention(qT, k, vT, tq=512, tk=8192, ts=256):
    b, h, _, s = qT.shape
    assert s % tk == 0 and tk % ts == 0 and s % (2 * tq) == 0 and s // tq >= 4
    return pl.pallas_call(
        functools.partial(_attn_kernel, tq=tq, tk=tk, ts=ts),
        grid=(b, h),
        in_specs=[
            pl.BlockSpec((1, 1, QK_PAD, s), lambda bi, hi: (bi, hi, 0, 0)),
            pl.BlockSpec((1, 1, s, QK_PAD), lambda bi, hi: (bi, hi, 0, 0)),
            pl.BlockSpec((1, 1, VT_ROWS, s), lambda bi, hi: (bi, hi, 0, 0)),
        ],
        out_specs=pl.BlockSpec((1, s, V_DIM), lambda bi, hi: (bi, 0, hi)),
        out_shape=jax.ShapeDtypeStruct((b, s, h * V_DIM), BF16),
        scratch_shapes=[pltpu.VMEM((s, tq), F32), pltpu.VMEM((s, tq), F32), pltpu.VMEM((VT_ROWS, tq), F32)],
        compiler_params=pltpu.CompilerParams(
            dimension_semantics=("parallel", "parallel"), vmem_limit_bytes=VMEM_LIMIT_BIG),
        name="attention",
    )(qT, k, vT)


def _merge_kernel(x_ref, lng_ref, lnb_ref, a_ref, y_ref, gc_ref, bc_ref, wa_ref, wc_ref, g1_ref, b1_ref, o_ref):
    x0 = _layer_norm(x_ref[...], lng_ref[...], lnb_ref[...])
    yn = _layer_norm(y_ref[...].reshape(y_ref.shape[0], CONV_CH), gc_ref[...], bc_ref[...])
    conv_out = (yn * jax.nn.sigmoid(yn)).astype(BF16)
    mix = (jnp.dot(a_ref[...], wa_ref[...], preferred_element_type=F32)
           + jnp.dot(conv_out, wc_ref[...], preferred_element_type=F32))
    o_ref[...] = _layer_norm(ALPHA * x0 + mix, g1_ref[...], b1_ref[...])


def _merge(x2d, ln_g, ln_b, attn2d, y2d, gc, bc, wa, wc, g1, b1, tm=512):
    t, d = x2d.shape
    assert t % tm == 0
    row = lambda i: (i, 0)
    return pl.pallas_call(
        _merge_kernel,
        grid=(t // tm,),
        in_specs=[
            pl.BlockSpec((tm, d), row), _const_spec((1, d)), _const_spec((1, d)),
            pl.BlockSpec((tm, MLA_WIDTH), row), pl.BlockSpec((tm, SUB, LANE), lambda i: (i, 0, 0)),
            _const_spec((1, CONV_CH)), _const_spec((1, CONV_CH)),
            _const_spec(wa.shape), _const_spec(wc.shape), _const_spec((1, d)), _const_spec((1, d)),
        ],
        out_specs=pl.BlockSpec((tm, d), row),
        out_shape=jax.ShapeDtypeStruct((t, d), F32),
        compiler_params=pltpu.CompilerParams(
            dimension_semantics=("parallel",), vmem_limit_bytes=VMEM_LIMIT),
        name="merge",
    )(x2d, ln_g.reshape(1, d), ln_b.reshape(1, d), attn2d, y2d, gc.reshape(1, CONV_CH), bc.reshape(1, CONV_CH),
      wa, wc, g1.reshape(1, d), b1.reshape(1, d))


def _ffn_kernel(x_ref, w1_ref, w2_ref, g_ref, b_ref, o_ref, xb_ref):
    j = pl.program_id(1)

    @pl.when(j == 0)
    def _():
        xb_ref[...] = x_ref[...].astype(BF16)
        o_ref[...] = jnp.zeros(o_ref.shape, F32)

    h = jnp.dot(xb_ref[...], w1_ref[...], preferred_element_type=F32)
    h = jnp.maximum(h, 0.0)
    o_ref[...] += jnp.dot((h * h).astype(BF16), w2_ref[...], preferred_element_type=F32)

    @pl.when(j == pl.num_programs(1) - 1)
    def _():
        o_ref[...] = _layer_norm(ALPHA * x_ref[...] + o_ref[...], g_ref[...], b_ref[...])


def _ffn(x1, w1, w2, g2, b2, tm=512, tf=2048):
    t, d = x1.shape
    f = w1.shape[1]
    assert t % tm == 0 and f % tf == 0
    return pl.pallas_call(
        _ffn_kernel,
        grid=(t // tm, f // tf),
        in_specs=[
            pl.BlockSpec((tm, d), lambda i, j: (i, 0)),
            pl.BlockSpec((d, tf), lambda i, j: (0, j)),
            pl.BlockSpec((tf, d), lambda i, j: (j, 0)),
            _const_spec((1, d)), _const_spec((1, d)),
        ],
        out_specs=pl.BlockSpec((tm, d), lambda i, j: (i, 0)),
        out_shape=jax.ShapeDtypeStruct((t, d), F32),
        scratch_shapes=[pltpu.VMEM((tm, d), BF16)],
        compiler_params=pltpu.CompilerParams(
            dimension_semantics=("parallel", "arbitrary"), vmem_limit_bytes=VMEM_LIMIT_BIG),
        name="ffn",
    )(x1, w1, w2, g2.reshape(1, d), b2.reshape(1, d))


def kernel(x, positions, ln_in_g, ln_in_b, w_in, g_cq, w_uq, g_ckv, w_uk, w_uv, conv_w, conv_b,
           g_conv_ln, b_conv_ln, w_out, g_ln1, b_ln1, w_ff1, w_ff2, g_ln2, b_ln2):
    b, s, d = x.shape
    w_in0 = w_in[0].astype(BF16)
    o_kr = Q_RANK + KV_RANK
    wc = w_in0[:, :o_kr]
    wkrT = w_in0[:, o_kr:o_kr + ROPE].T
    wconv = w_in0[:, o_kr + ROPE:]
    wuqT = w_uq[0].T.astype(BF16)
    wuk = w_uk[0].astype(BF16)
    wuvT = w_uv[0].T.astype(BF16)
    wa = w_out[0, :MLA_WIDTH].astype(BF16)
    wcv = w_out[0, MLA_WIDTH:].astype(BF16)
    w1 = w_ff1[0].astype(BF16)
    w2 = w_ff2[0].astype(BF16)

    cosT, sinT = _rope_tables(positions)
    qT, k, vT, u = _in_proj(x, ln_in_g, ln_in_b, wc, wkrT, wconv, g_cq[0], g_ckv[0],
                            wuqT, wuk, wuvT, cosT, sinT)
    y = _conv_module(u, conv_w[0], conv_b[0])
    attn = _attention(qT, k, vT)
    x2d = x.reshape(b * s, d)
    x1 = _merge(x2d, ln_in_g, ln_in_b, attn.reshape(b * s, MLA_WIDTH), y.reshape(b * s, SUB, LANE),
                g_conv_ln[0], b_conv_ln[0], wa, wcv, g_ln1[0], b_ln1[0])
    out = _ffn(x1, w1, w2, g_ln2[0], b_ln2[0])
    return out.reshape(b, s, d)
```

```python
import functools
import math

import jax
import jax.numpy as jnp
from jax import lax
from jax.experimental import pallas as pl
from jax.experimental.pallas import tpu as pltpu

F32 = jnp.float32
BF16 = jnp.bfloat16

D_MODEL = 2048
HEADS = 8
NOPE = 128
ROPE = 64
HALF = ROPE // 2
QK = NOPE + ROPE
QK_PAD = 256
V_DIM = 128
VT_ROWS = V_DIM + 16
Q_RANK = 512
KV_RANK = 512
MLA_WIDTH = HEADS * V_DIM
CONV_CH = D_MODEL - MLA_WIDTH
CONV_W = 31
CONV_PAD = CONV_W // 2
HALO = 16
D_FF = 4 * D_MODEL
ROPE_BASE = 10000.0
LN_EPS = 1e-5
RMS_EPS = 1e-6
ALPHA = 2.0 ** 0.25
Q_SCALE = (QK ** -0.5) * math.log2(math.e)

SUB = 8
LANE = 128
VMEM_LIMIT = 56 * 1024 * 1024
VMEM_LIMIT_BIG = 60 * 1024 * 1024

NT_DIMS = (((1,), (1,)), ((), ()))


def _const_spec(shape):
    nd = len(shape)
    return pl.BlockSpec(shape, lambda *_: (0,) * nd, pipeline_mode=pl.Buffered(1))


def _layer_norm(x, g, b):
    mu = jnp.mean(x, axis=-1, keepdims=True)
    xc = x - mu
    var = jnp.mean(xc * xc, axis=-1, keepdims=True)
    return xc * lax.rsqrt(var + LN_EPS) * g + b


def _rope_kernel(pos_ref, invf_ref, cos_ref, sin_ref):
    ang = invf_ref[...] * pos_ref[0].astype(F32)
    cos_ref[0] = jnp.cos(ang)
    sin_ref[0] = jnp.sin(ang)


def _rope_tables(positions, ts=2048):
    b, s = positions.shape
    assert s % ts == 0
    inv_freq = ROPE_BASE ** (-jnp.arange(HALF, dtype=F32) * (2.0 / ROPE))
    out = jax.ShapeDtypeStruct((b, HALF, s), F32)
    return pl.pallas_call(
        _rope_kernel,
        grid=(b, s // ts),
        in_specs=[pl.BlockSpec((1, 1, ts), lambda i, j: (i, 0, j)),
                  pl.BlockSpec((HALF, 1), lambda i, j: (0, 0))],
        out_specs=[pl.BlockSpec((1, HALF, ts), lambda i, j: (i, 0, j))] * 2,
        out_shape=[out, out],
        name="rope_tables",
    )(positions.reshape(b, 1, s), inv_freq.reshape(HALF, 1))


def _inproj_kernel(x_ref, lng_ref, lnb_ref, wc_ref, wkrT_ref, wconv_ref, gcq_ref, gckv_ref,
                   wuqT_ref, wuk_ref, wuvT_ref, cos_ref, sin_ref,
                   qT_ref, k_ref, vT_ref, u_ref):
    tm = x_ref.shape[1]
    xb = _layer_norm(x_ref[0], lng_ref[...], lnb_ref[...]).astype(BF16)
    cos = cos_ref[0]
    sin = sin_ref[0]

    ag = jnp.dot(xb, wconv_ref[...], preferred_element_type=F32)
    u = ag[:, :CONV_CH] * jax.nn.sigmoid(ag[:, CONV_CH:])
    u_ref[0] = u.reshape(tm, SUB, LANE)

    c = jnp.dot(xb, wc_ref[...], preferred_element_type=F32)
    c_q = c[:, :Q_RANK]
    c_kv = c[:, Q_RANK:]
    rq = lax.rsqrt(jnp.mean(c_q * c_q, axis=-1, keepdims=True) + RMS_EPS) * Q_SCALE
    cqn = (c_q * rq * gcq_ref[...]).astype(BF16)
    rkv = lax.rsqrt(jnp.mean(c_kv * c_kv, axis=-1, keepdims=True) + RMS_EPS)
    ckvn = (c_kv * rkv * gckv_ref[...]).astype(BF16)

    qT = lax.dot_general(wuqT_ref[...], cqn, NT_DIMS, preferred_element_type=F32)
    zeros_q = jnp.zeros((QK_PAD - QK, tm), BF16)
    for h in range(HEADS):
        base = h * QK
        r1 = qT[base + NOPE:base + NOPE + HALF]
        r2 = qT[base + NOPE + HALF:base + QK]
        qT_ref[0, h, 0:NOPE, :] = qT[base:base + NOPE].astype(BF16)
        qT_ref[0, h, NOPE:NOPE + HALF, :] = (r1 * cos - r2 * sin).astype(BF16)
        qT_ref[0, h, NOPE + HALF:QK, :] = (r2 * cos + r1 * sin).astype(BF16)
        qT_ref[0, h, QK:QK_PAD, :] = zeros_q

    krT = lax.dot_general(wkrT_ref[...], xb, NT_DIMS, preferred_element_type=F32)
    k1 = krT[:HALF]
    k2 = krT[HALF:]
    kr_fullT = jnp.concatenate(
        [k1 * cos - k2 * sin, k2 * cos + k1 * sin, jnp.zeros((QK_PAD - QK, tm), F32)], axis=0)
    kr = kr_fullT.T.astype(BF16)

    kn = jnp.dot(ckvn, wuk_ref[...], preferred_element_type=F32)
    vT = lax.dot_general(wuvT_ref[...], ckvn, NT_DIMS, preferred_element_type=F32)
    row = lax.broadcasted_iota(jnp.int32, (VT_ROWS - V_DIM, tm), 0)
    ones_row = jnp.where(row == 0, 1.0, 0.0).astype(BF16)
    for h in range(HEADS):
        k_ref[0, h, :, 0:NOPE] = kn[:, h * NOPE:(h + 1) * NOPE].astype(BF16)
        k_ref[0, h, :, NOPE:QK_PAD] = kr
        vT_ref[0, h, 0:V_DIM, :] = vT[h * V_DIM:(h + 1) * V_DIM].astype(BF16)
        vT_ref[0, h, V_DIM:VT_ROWS, :] = ones_row


def _in_proj(x, ln_g, ln_b, wc, wkrT, wconv, g_cq, g_ckv, wuqT, wuk, wuvT, cosT, sinT, tm=512):
    b, s, d = x.shape
    assert s % tm == 0 and d == D_MODEL and wconv.shape == (d, 2 * CONV_CH)
    grid = (b, s // tm)
    out_shape = [
        jax.ShapeDtypeStruct((b, HEADS, QK_PAD, s), BF16),
        jax.ShapeDtypeStruct((b, HEADS, s, QK_PAD), BF16),
        jax.ShapeDtypeStruct((b, HEADS, VT_ROWS, s), BF16),
        jax.ShapeDtypeStruct((b, s, SUB, LANE), F32),
    ]
    in_specs = [
        pl.BlockSpec((1, tm, d), lambda i, j: (i, j, 0)),
        _const_spec((1, d)), _const_spec((1, d)),
        _const_spec(wc.shape), _const_spec(wkrT.shape), _const_spec(wconv.shape),
        _const_spec((1, Q_RANK)), _const_spec((1, KV_RANK)),
        _const_spec(wuqT.shape), _const_spec(wuk.shape), _const_spec(wuvT.shape),
        pl.BlockSpec((1, HALF, tm), lambda i, j: (i, 0, j)),
        pl.BlockSpec((1, HALF, tm), lambda i, j: (i, 0, j)),
    ]
    out_specs = [
        pl.BlockSpec((1, HEADS, QK_PAD, tm), lambda i, j: (i, 0, 0, j)),
        pl.BlockSpec((1, HEADS, tm, QK_PAD), lambda i, j: (i, 0, j, 0)),
        pl.BlockSpec((1, HEADS, VT_ROWS, tm), lambda i, j: (i, 0, 0, j)),
        pl.BlockSpec((1, tm, SUB, LANE), lambda i, j: (i, j, 0, 0)),
    ]
    return pl.pallas_call(
        _inproj_kernel, grid=grid, in_specs=in_specs, out_specs=out_specs, out_shape=out_shape,
        compiler_params=pltpu.CompilerParams(
            dimension_semantics=("parallel", "parallel"), vmem_limit_bytes=VMEM_LIMIT),
        name="in_proj",
    )(x, ln_g.reshape(1, d), ln_b.reshape(1, d), wc, wkrT, wconv,
      g_cq.reshape(1, Q_RANK), g_ckv.reshape(1, KV_RANK), wuqT, wuk, wuvT, cosT, sinT)


CONV_TB = 16


def _conv_kernel(u_ref, prev_ref, next_ref, w_ref, b_ref, y_ref, ext_ref):
    tc = u_ref.shape[1]
    i = pl.program_id(1)
    last = pl.num_programs(1) - 1
    ext_ref[0:HALO] = jnp.where(i > 0, prev_ref[0], 0.0)
    ext_ref[HALO:HALO + tc] = u_ref[0]
    ext_ref[HALO + tc:HALO + tc + HALO] = jnp.where(i < last, next_ref[0], 0.0)
    off = HALO - CONV_PAD
    bias = b_ref[...]

    for t0 in range(0, tc, CONV_TB):
        acc = jnp.zeros((CONV_TB, SUB, LANE), F32)
        for k in range(CONV_W):
            acc = acc + w_ref[k] * ext_ref[t0 + off + k:t0 + off + k + CONV_TB]
        y_ref[0, t0:t0 + CONV_TB] = acc + bias


def _conv_module(u4, conv_w, conv_b, tc=512):
    b, s = u4.shape[:2]
    assert s % tc == 0 and tc % CONV_TB == 0 and tc % HALO == 0 and HALO >= CONV_PAD
    nh = tc // HALO
    n_halo_blocks = s // HALO
    return pl.pallas_call(
        _conv_kernel,
        grid=(b, s // tc),
        in_specs=[
            pl.BlockSpec((1, tc, SUB, LANE), lambda i, j: (i, j, 0, 0)),
            pl.BlockSpec((1, HALO, SUB, LANE), lambda i, j: (i, jnp.maximum(j * nh - 1, 0), 0, 0)),
            pl.BlockSpec((1, HALO, SUB, LANE),
                         lambda i, j: (i, jnp.minimum((j + 1) * nh, n_halo_blocks - 1), 0, 0)),
            _const_spec((CONV_W, SUB, LANE)), _const_spec((SUB, LANE)),
        ],
        out_specs=pl.BlockSpec((1, tc, SUB, LANE), lambda i, j: (i, j, 0, 0)),
        out_shape=jax.ShapeDtypeStruct((b, s, SUB, LANE), F32),
        scratch_shapes=[pltpu.VMEM((tc + 2 * HALO, SUB, LANE), F32)],
        compiler_params=pltpu.CompilerParams(dimension_semantics=("parallel", "parallel")),
        name="conv_module",
    )(u4, u4, u4, conv_w.reshape(CONV_W, SUB, LANE), conv_b.reshape(SUB, LANE))


def _attn_kernel(qT_ref, k_ref, vT_ref, o_ref, sa_ref, sb_ref, acc_ref, *, tq, tk, ts):
    s_len = k_ref.shape[2]
    nq = s_len // tq
    nchunk = s_len // tk
    neg = jnp.full((SUB, tq), -jnp.inf, F32)

    def q_block(q):
        return qT_ref[0, 0, :, pl.ds(pl.multiple_of(q * tq, tq), tq)]

    def score_chunk(start, qT, s_ref, mpart):
        s = jnp.dot(k_ref[0, 0, pl.ds(start, ts), :], qT, preferred_element_type=F32)
        s_ref[pl.ds(start, ts), :] = s
        return jnp.maximum(mpart, jnp.max(s.reshape(ts // SUB, SUB, tq), axis=0))

    def weight_chunk(start, s_ref, m):
        p = jnp.exp2(s_ref[pl.ds(start, ts), :] - m).astype(BF16)
        acc_ref[...] += jnp.dot(vT_ref[0, 0, :, pl.ds(start, ts)], p, preferred_element_type=F32)

    def finish(q):
        o = acc_ref[0:V_DIM, :] / acc_ref[V_DIM:V_DIM + 1, :]
        o_ref[0, pl.ds(pl.multiple_of(q * tq, tq), tq), :] = o.T.astype(o_ref.dtype)

    def sweep(q_next, s_next_ref, q_cur, s_cur_ref, mpart_cur):
        qT = None if q_next is None else q_block(q_next)
        if q_cur is not None:
            m = jnp.max(mpart_cur, axis=0, keepdims=True)
            acc_ref[...] = jnp.zeros(acc_ref.shape, F32)

        def body(c, mpart):
            for j in range(tk // ts):
                start = pl.multiple_of(c * tk + j * ts, ts)
                if q_cur is not None:
                    weight_chunk(start, s_cur_ref, m)
                if q_next is not None:
                    mpart = score_chunk(start, qT, s_next_ref, mpart)
            return mpart

        mpart_next = lax.fori_loop(0, nchunk, body, neg)
        if q_cur is not None:
            finish(q_cur)
        return mpart_next

    mpart_a = sweep(0, sa_ref, None, None, None)

    def pair(i, mpart_a):
        q = 2 * i
        mpart_b = sweep(q + 1, sb_ref, q, sa_ref, mpart_a)
        return sweep(q + 2, sa_ref, q + 1, sb_ref, mpart_b)

    mpart_a = lax.fori_loop(0, nq // 2 - 1, pair, mpart_a)
    mpart_b = sweep(nq - 1, sb_ref, nq - 2, sa_ref, mpart_a)
    sweep(None, None, nq - 1, sb_ref, mpart_b)


def _attention(qT, k, vT, tq=512, tk=8192, ts=256):
    b, h, _, s = qT.shape
    assert s % tk == 0 and tk % ts == 0 and s % (2 * tq) == 0 and s // tq >= 4
    return pl.pallas_call(
        functools.partial(_attn_kernel, tq=tq, tk=tk, ts=ts),
        grid=(b, h),
        in_specs=[
            pl.BlockSpec((1, 1, QK_PAD, s), lambda bi, hi: (bi, hi, 0, 0)),
            pl.BlockSpec((1, 1, s, QK_PAD), lambda bi, hi: (bi, hi, 0, 0)),
            pl.BlockSpec((1, 1, VT_ROWS, s), lambda bi, hi: (bi, hi, 0, 0)),
        ],
        out_specs=pl.BlockSpec((1, s, V_DIM), lambda bi, hi: (bi, 0, hi)),
        out_shape=jax.ShapeDtypeStruct((b, s, h * V_DIM), BF16),
        scratch_shapes=[pltpu.VMEM((s, tq), F32), pltpu.VMEM((s, tq), F32), pltpu.VMEM((VT_ROWS, tq), F32)],
        compiler_params=pltpu.CompilerParams(
            dimension_semantics=("parallel", "parallel"), vmem_limit_bytes=VMEM_LIMIT_BIG),
        name="attention",
    )(qT, k, vT)


def _merge_kernel(x_ref, lng_ref, lnb_ref, a_ref, y_ref, gc_ref, bc_ref, wa_ref, wc_ref, g1_ref, b1_ref,
                  w1f_ref, w2f_ref, o_ref, w1b_ref, w2b_ref):
    w1b_ref[...] = w1f_ref[...].astype(BF16)
    w2b_ref[...] = w2f_ref[...].astype(BF16)
    x0 = _layer_norm(x_ref[...], lng_ref[...], lnb_ref[...])
    yn = _layer_norm(y_ref[...].reshape(y_ref.shape[0], CONV_CH), gc_ref[...], bc_ref[...])
    conv_out = (yn * jax.nn.sigmoid(yn)).astype(BF16)
    mix = (jnp.dot(a_ref[...], wa_ref[...], preferred_element_type=F32)
           + jnp.dot(conv_out, wc_ref[...], preferred_element_type=F32))
    o_ref[...] = _layer_norm(ALPHA * x0 + mix, g1_ref[...], b1_ref[...])


def _merge(x2d, ln_g, ln_b, attn2d, y2d, gc, bc, wa, wc, g1, b1, w1f, w2f, tm=512):
    t, d = x2d.shape
    assert t % tm == 0
    steps = t // tm
    r1, r2 = w1f.shape[0] // steps, w2f.shape[0] // steps
    bf16_rows = 2 * SUB
    assert r1 * steps == w1f.shape[0] and r2 * steps == w2f.shape[0] and r1 % bf16_rows == 0 and r2 % bf16_rows == 0
    row = lambda i: (i, 0)
    return pl.pallas_call(
        _merge_kernel,
        grid=(steps,),
        in_specs=[
            pl.BlockSpec((tm, d), row), _const_spec((1, d)), _const_spec((1, d)),
            pl.BlockSpec((tm, MLA_WIDTH), row), pl.BlockSpec((tm, SUB, LANE), lambda i: (i, 0, 0)),
            _const_spec((1, CONV_CH)), _const_spec((1, CONV_CH)),
            _const_spec(wa.shape), _const_spec(wc.shape), _const_spec((1, d)), _const_spec((1, d)),
            pl.BlockSpec((r1, w1f.shape[1]), row), pl.BlockSpec((r2, w2f.shape[1]), row),
        ],
        out_specs=[pl.BlockSpec((tm, d), row),
                   pl.BlockSpec((r1, w1f.shape[1]), row), pl.BlockSpec((r2, w2f.shape[1]), row)],
        out_shape=[jax.ShapeDtypeStruct((t, d), F32),
                   jax.ShapeDtypeStruct(w1f.shape, BF16), jax.ShapeDtypeStruct(w2f.shape, BF16)],
        compiler_params=pltpu.CompilerParams(
            dimension_semantics=("parallel",), vmem_limit_bytes=VMEM_LIMIT),
        name="merge",
    )(x2d, ln_g.reshape(1, d), ln_b.reshape(1, d), attn2d, y2d, gc.reshape(1, CONV_CH), bc.reshape(1, CONV_CH),
      wa, wc, g1.reshape(1, d), b1.reshape(1, d), w1f, w2f)


def _ffn_kernel(x_ref, w1_ref, w2_ref, g_ref, b_ref, o_ref, xb_ref):
    j = pl.program_id(1)

    @pl.when(j == 0)
    def _():
        xb_ref[...] = x_ref[...].astype(BF16)
        o_ref[...] = jnp.zeros(o_ref.shape, F32)

    h = jnp.dot(xb_ref[...], w1_ref[...], preferred_element_type=F32)
    h = jnp.maximum(h, 0.0)
    o_ref[...] += jnp.dot((h * h).astype(BF16), w2_ref[...], preferred_element_type=F32)

    @pl.when(j == pl.num_programs(1) - 1)
    def _():
        o_ref[...] = _layer_norm(ALPHA * x_ref[...] + o_ref[...], g_ref[...], b_ref[...])


def _ffn(x1, w1, w2, g2, b2, tm=512, tf=2048):
    t, d = x1.shape
    f = w1.shape[1]
    assert t % tm == 0 and f % tf == 0
    return pl.pallas_call(
        _ffn_kernel,
        grid=(t // tm, f // tf),
        in_specs=[
            pl.BlockSpec((tm, d), lambda i, j: (i, 0)),
            pl.BlockSpec((d, tf), lambda i, j: (0, j)),
            pl.BlockSpec((tf, d), lambda i, j: (j, 0)),
            _const_spec((1, d)), _const_spec((1, d)),
        ],
        out_specs=pl.BlockSpec((tm, d), lambda i, j: (i, 0)),
        out_shape=jax.ShapeDtypeStruct((t, d), F32),
        scratch_shapes=[pltpu.VMEM((tm, d), BF16)],
        compiler_params=pltpu.CompilerParams(
            dimension_semantics=("parallel", "arbitrary"), vmem_limit_bytes=VMEM_LIMIT_BIG),
        name="ffn",
    )(x1, w1, w2, g2.reshape(1, d), b2.reshape(1, d))


def kernel(x, positions, ln_in_g, ln_in_b, w_in, g_cq, w_uq, g_ckv, w_uk, w_uv, conv_w, conv_b,
           g_conv_ln, b_conv_ln, w_out, g_ln1, b_ln1, w_ff1, w_ff2, g_ln2, b_ln2):
    b, s, d = x.shape
    w_in0 = w_in[0].astype(BF16)
    o_kr = Q_RANK + KV_RANK
    wc = w_in0[:, :o_kr]
    wkrT = w_in0[:, o_kr:o_kr + ROPE].T
    wconv = w_in0[:, o_kr + ROPE:]
    wuqT = w_uq[0].T.astype(BF16)
    wuk = w_uk[0].astype(BF16)
    wuvT = w_uv[0].T.astype(BF16)
    wa = w_out[0, :MLA_WIDTH].astype(BF16)
    wcv = w_out[0, MLA_WIDTH:].astype(BF16)

    cosT, sinT = _rope_tables(positions)
    qT, k, vT, u = _in_proj(x, ln_in_g, ln_in_b, wc, wkrT, wconv, g_cq[0], g_ckv[0],
                            wuqT, wuk, wuvT, cosT, sinT)
    y = _conv_module(u, conv_w[0], conv_b[0])
    attn = _attention(qT, k, vT)
    x2d = x.reshape(b * s, d)
    x1, w1, w2 = _merge(x2d, ln_in_g, ln_in_b, attn.reshape(b * s, MLA_WIDTH), y.reshape(b * s, SUB, LANE),
                        g_conv_ln[0], b_conv_ln[0], wa, wcv, g_ln1[0], b_ln1[0], w_ff1[0], w_ff2[0])
    out = _ffn(x1, w1, w2, g_ln2[0], b_ln2[0])
    return out.reshape(b, s, d)
```

```python
import functools
import math

import jax
import jax.numpy as jnp
from jax import lax
from jax.experimental import pallas as pl
from jax.experimental.pallas import tpu as pltpu

F32 = jnp.float32
BF16 = jnp.bfloat16

D_MODEL = 2048
HEADS = 8
NOPE = 128
ROPE = 64
HALF = ROPE // 2
QK = NOPE + ROPE
QK_PAD = 256
V_DIM = 128
VT_ROWS = V_DIM + 16
Q_RANK = 512
KV_RANK = 512
MLA_WIDTH = HEADS * V_DIM
CONV_CH = D_MODEL - MLA_WIDTH
CONV_W = 31
CONV_PAD = CONV_W // 2
HALO = 16
D_FF = 4 * D_MODEL
ROPE_BASE = 10000.0
LN_EPS = 1e-5
RMS_EPS = 1e-6
ALPHA = 2.0 ** 0.25
Q_SCALE = (QK ** -0.5) * math.log2(math.e)

SUB = 8
LANE = 128
VMEM_LIMIT = 56 * 1024 * 1024
VMEM_LIMIT_BIG = 60 * 1024 * 1024

NT_DIMS = (((1,), (1,)), ((), ()))


def _const_spec(shape):
    nd = len(shape)
    return pl.BlockSpec(shape, lambda *_: (0,) * nd, pipeline_mode=pl.Buffered(1))


def _layer_norm(x, g, b):
    mu = jnp.mean(x, axis=-1, keepdims=True)
    xc = x - mu
    var = jnp.mean(xc * xc, axis=-1, keepdims=True)
    return xc * lax.rsqrt(var + LN_EPS) * g + b


def _slab_specs(weights, steps, step_of):
    specs, shapes = [], []
    for w in weights:
        rows = w.shape[0] // steps
        assert rows * steps == w.shape[0] and rows % (2 * SUB) == 0
        specs.append(pl.BlockSpec((rows, w.shape[1]), lambda *g: (step_of(*g), 0)))
        shapes.append(jax.ShapeDtypeStruct(w.shape, BF16))
    return specs, shapes


def _round_slabs(f32_refs, bf16_refs):
    for src, dst in zip(f32_refs, bf16_refs):
        dst[...] = src[...].astype(BF16)


def _rope_kernel(pos_ref, invf_ref, *refs):
    n = (len(refs) - 2) // 2
    w_f32, (cos_ref, sin_ref), w_bf16 = refs[:n], refs[n:n + 2], refs[n + 2:]
    ang = invf_ref[...] * pos_ref[0].astype(F32)
    cos_ref[0] = jnp.cos(ang)
    sin_ref[0] = jnp.sin(ang)
    _round_slabs(w_f32, w_bf16)


def _rope_tables(positions, weights, ts=2048):
    b, s = positions.shape
    assert s % ts == 0
    nj = s // ts
    inv_freq = ROPE_BASE ** (-jnp.arange(HALF, dtype=F32) * (2.0 / ROPE))
    out = jax.ShapeDtypeStruct((b, HALF, s), F32)
    w_specs, w_shapes = _slab_specs(weights, b * nj, lambda i, j: i * nj + j)
    return pl.pallas_call(
        _rope_kernel,
        grid=(b, nj),
        in_specs=[pl.BlockSpec((1, 1, ts), lambda i, j: (i, 0, j)),
                  pl.BlockSpec((HALF, 1), lambda i, j: (0, 0))] + w_specs,
        out_specs=[pl.BlockSpec((1, HALF, ts), lambda i, j: (i, 0, j))] * 2 + w_specs,
        out_shape=[out, out] + w_shapes,
        compiler_params=pltpu.CompilerParams(
            dimension_semantics=("parallel", "parallel"), vmem_limit_bytes=VMEM_LIMIT),
        name="rope_tables",
    )(positions.reshape(b, 1, s), inv_freq.reshape(HALF, 1), *weights)


def _inproj_kernel(x_ref, lng_ref, lnb_ref, wc_ref, wkrT_ref, wconv_ref, gcq_ref, gckv_ref,
                   wuqT_ref, wuk_ref, wuvT_ref, cos_ref, sin_ref,
                   qT_ref, k_ref, vT_ref, u_ref):
    tm = x_ref.shape[1]
    xb = _layer_norm(x_ref[0], lng_ref[...], lnb_ref[...]).astype(BF16)
    cos = cos_ref[0]
    sin = sin_ref[0]

    ag = jnp.dot(xb, wconv_ref[...], preferred_element_type=F32)
    u = ag[:, :CONV_CH] * jax.nn.sigmoid(ag[:, CONV_CH:])
    u_ref[0] = u.reshape(tm, SUB, LANE)

    c = jnp.dot(xb, wc_ref[...], preferred_element_type=F32)
    c_q = c[:, :Q_RANK]
    c_kv = c[:, Q_RANK:]
    rq = lax.rsqrt(jnp.mean(c_q * c_q, axis=-1, keepdims=True) + RMS_EPS) * Q_SCALE
    cqn = (c_q * rq * gcq_ref[...]).astype(BF16)
    rkv = lax.rsqrt(jnp.mean(c_kv * c_kv, axis=-1, keepdims=True) + RMS_EPS)
    ckvn = (c_kv * rkv * gckv_ref[...]).astype(BF16)

    qT = lax.dot_general(wuqT_ref[...], cqn, NT_DIMS, preferred_element_type=F32)
    zeros_q = jnp.zeros((QK_PAD - QK, tm), BF16)
    for h in range(HEADS):
        base = h * QK
        r1 = qT[base + NOPE:base + NOPE + HALF]
        r2 = qT[base + NOPE + HALF:base + QK]
        qT_ref[0, h, 0:NOPE, :] = qT[base:base + NOPE].astype(BF16)
        qT_ref[0, h, NOPE:NOPE + HALF, :] = (r1 * cos - r2 * sin).astype(BF16)
        qT_ref[0, h, NOPE + HALF:QK, :] = (r2 * cos + r1 * sin).astype(BF16)
        qT_ref[0, h, QK:QK_PAD, :] = zeros_q

    krT = lax.dot_general(wkrT_ref[...], xb, NT_DIMS, preferred_element_type=F32)
    k1 = krT[:HALF]
    k2 = krT[HALF:]
    kr_fullT = jnp.concatenate(
        [k1 * cos - k2 * sin, k2 * cos + k1 * sin, jnp.zeros((QK_PAD - QK, tm), F32)], axis=0)
    kr = kr_fullT.T.astype(BF16)

    kn = jnp.dot(ckvn, wuk_ref[...], preferred_element_type=F32)
    vT = lax.dot_general(wuvT_ref[...], ckvn, NT_DIMS, preferred_element_type=F32)
    row = lax.broadcasted_iota(jnp.int32, (VT_ROWS - V_DIM, tm), 0)
    ones_row = jnp.where(row == 0, 1.0, 0.0).astype(BF16)
    for h in range(HEADS):
        k_ref[0, h, :, 0:NOPE] = kn[:, h * NOPE:(h + 1) * NOPE].astype(BF16)
        k_ref[0, h, :, NOPE:QK_PAD] = kr
        vT_ref[0, h, 0:V_DIM, :] = vT[h * V_DIM:(h + 1) * V_DIM].astype(BF16)
        vT_ref[0, h, V_DIM:VT_ROWS, :] = ones_row


def _in_proj(x, ln_g, ln_b, wc, wkrT, wconv, g_cq, g_ckv, wuqT, wuk, wuvT, cosT, sinT, tm=512):
    b, s, d = x.shape
    assert s % tm == 0 and d == D_MODEL and wconv.shape == (d, 2 * CONV_CH)
    grid = (b, s // tm)
    out_shape = [
        jax.ShapeDtypeStruct((b, HEADS, QK_PAD, s), BF16),
        jax.ShapeDtypeStruct((b, HEADS, s, QK_PAD), BF16),
        jax.ShapeDtypeStruct((b, HEADS, VT_ROWS, s), BF16),
        jax.ShapeDtypeStruct((b, s, SUB, LANE), F32),
    ]
    in_specs = [
        pl.BlockSpec((1, tm, d), lambda i, j: (i, j, 0)),
        _const_spec((1, d)), _const_spec((1, d)),
        _const_spec(wc.shape), _const_spec(wkrT.shape), _const_spec(wconv.shape),
        _const_spec((1, Q_RANK)), _const_spec((1, KV_RANK)),
        _const_spec(wuqT.shape), _const_spec(wuk.shape), _const_spec(wuvT.shape),
        pl.BlockSpec((1, HALF, tm), lambda i, j: (i, 0, j)),
        pl.BlockSpec((1, HALF, tm), lambda i, j: (i, 0, j)),
    ]
    out_specs = [
        pl.BlockSpec((1, HEADS, QK_PAD, tm), lambda i, j: (i, 0, 0, j)),
        pl.BlockSpec((1, HEADS, tm, QK_PAD), lambda i, j: (i, 0, j, 0)),
        pl.BlockSpec((1, HEADS, VT_ROWS, tm), lambda i, j: (i, 0, 0, j)),
        pl.BlockSpec((1, tm, SUB, LANE), lambda i, j: (i, j, 0, 0)),
    ]
    return pl.pallas_call(
        _inproj_kernel, grid=grid, in_specs=in_specs, out_specs=out_specs, out_shape=out_shape,
        compiler_params=pltpu.CompilerParams(
            dimension_semantics=("parallel", "parallel"), vmem_limit_bytes=VMEM_LIMIT),
        name="in_proj",
    )(x, ln_g.reshape(1, d), ln_b.reshape(1, d), wc, wkrT, wconv,
      g_cq.reshape(1, Q_RANK), g_ckv.reshape(1, KV_RANK), wuqT, wuk, wuvT, cosT, sinT)


CONV_TB = 16


def _conv_kernel(u_ref, prev_ref, next_ref, w_ref, b_ref, wo_f32_ref, y_ref, wo_bf16_ref, ext_ref):
    _round_slabs([wo_f32_ref], [wo_bf16_ref])
    tc = u_ref.shape[1]
    i = pl.program_id(1)
    last = pl.num_programs(1) - 1
    ext_ref[0:HALO] = jnp.where(i > 0, prev_ref[0], 0.0)
    ext_ref[HALO:HALO + tc] = u_ref[0]
    ext_ref[HALO + tc:HALO + tc + HALO] = jnp.where(i < last, next_ref[0], 0.0)
    off = HALO - CONV_PAD
    bias = b_ref[...]

    for t0 in range(0, tc, CONV_TB):
        acc = jnp.zeros((CONV_TB, SUB, LANE), F32)
        for k in range(CONV_W):
            acc = acc + w_ref[k] * ext_ref[t0 + off + k:t0 + off + k + CONV_TB]
        y_ref[0, t0:t0 + CONV_TB] = acc + bias


def _conv_module(u4, conv_w, conv_b, w_out_f32, tc=512):
    b, s = u4.shape[:2]
    assert s % tc == 0 and tc % CONV_TB == 0 and tc % HALO == 0 and HALO >= CONV_PAD
    nh = tc // HALO
    nj = s // tc
    n_halo_blocks = s // HALO
    w_specs, w_shapes = _slab_specs([w_out_f32], b * nj, lambda i, j: i * nj + j)
    return pl.pallas_call(
        _conv_kernel,
        grid=(b, nj),
        in_specs=[
            pl.BlockSpec((1, tc, SUB, LANE), lambda i, j: (i, j, 0, 0)),
            pl.BlockSpec((1, HALO, SUB, LANE), lambda i, j: (i, jnp.maximum(j * nh - 1, 0), 0, 0)),
            pl.BlockSpec((1, HALO, SUB, LANE),
                         lambda i, j: (i, jnp.minimum((j + 1) * nh, n_halo_blocks - 1), 0, 0)),
            _const_spec((CONV_W, SUB, LANE)), _const_spec((SUB, LANE)),
        ] + w_specs,
        out_specs=[pl.BlockSpec((1, tc, SUB, LANE), lambda i, j: (i, j, 0, 0))] + w_specs,
        out_shape=[jax.ShapeDtypeStruct((b, s, SUB, LANE), F32)] + w_shapes,
        scratch_shapes=[pltpu.VMEM((tc + 2 * HALO, SUB, LANE), F32)],
        compiler_params=pltpu.CompilerParams(dimension_semantics=("parallel", "parallel")),
        name="conv_module",
    )(u4, u4, u4, conv_w.reshape(CONV_W, SUB, LANE), conv_b.reshape(SUB, LANE), w_out_f32)


def _attn_kernel(qT_ref, k_ref, vT_ref, o_ref, sa_ref, sb_ref, acc_ref, *, tq, tk, ts):
    s_len = k_ref.shape[2]
    nq = s_len // tq
    nchunk = s_len // tk
    neg = jnp.full((SUB, tq), -jnp.inf, F32)

    def q_block(q):
        return qT_ref[0, 0, :, pl.ds(pl.multiple_of(q * tq, tq), tq)]

    def score_chunk(start, qT, s_ref, mpart):
        s = jnp.dot(k_ref[0, 0, pl.ds(start, ts), :], qT, preferred_element_type=F32)
        s_ref[pl.ds(start, ts), :] = s
        return jnp.maximum(mpart, jnp.max(s.reshape(ts // SUB, SUB, tq), axis=0))

    def weight_chunk(start, s_ref, m):
        p = jnp.exp2(s_ref[pl.ds(start, ts), :] - m).astype(BF16)
        acc_ref[...] += jnp.dot(vT_ref[0, 0, :, pl.ds(start, ts)], p, preferred_element_type=F32)

    def finish(q):
        o = acc_ref[0:V_DIM, :] / acc_ref[V_DIM:V_DIM + 1, :]
        o_ref[0, pl.ds(pl.multiple_of(q * tq, tq), tq), :] = o.T.astype(o_ref.dtype)

    def sweep(q_next, s_next_ref, q_cur, s_cur_ref, mpart_cur):
        qT = None if q_next is None else q_block(q_next)
        if q_cur is not None:
            m = jnp.max(mpart_cur, axis=0, keepdims=True)
            acc_ref[...] = jnp.zeros(acc_ref.shape, F32)

        def body(c, mpart):
            for j in range(tk // ts):
                start = pl.multiple_of(c * tk + j * ts, ts)
                if q_cur is not None:
                    weight_chunk(start, s_cur_ref, m)
                if q_next is not None:
                    mpart = score_chunk(start, qT, s_next_ref, mpart)
            return mpart

        mpart_next = lax.fori_loop(0, nchunk, body, neg)
        if q_cur is not None:
            finish(q_cur)
        return mpart_next

    mpart_a = sweep(0, sa_ref, None, None, None)

    def pair(i, mpart_a):
        q = 2 * i
        mpart_b = sweep(q + 1, sb_ref, q, sa_ref, mpart_a)
        return sweep(q + 2, sa_ref, q + 1, sb_ref, mpart_b)

    mpart_a = lax.fori_loop(0, nq // 2 - 1, pair, mpart_a)
    mpart_b = sweep(nq - 1, sb_ref, nq - 2, sa_ref, mpart_a)
    sweep(None, None, nq - 1, sb_ref, mpart_b)


def _attention(qT, k, vT, tq=512, tk=8192, ts=256):
    b, h, _, s = qT.shape
    assert s % tk == 0 and tk % ts == 0 and s % (2 * tq) == 0 and s // tq >= 4
    return pl.pallas_call(
        functools.partial(_attn_kernel, tq=tq, tk=tk, ts=ts),
        grid=(b, h),
        in_specs=[
            pl.BlockSpec((1, 1, QK_PAD, s), lambda bi, hi: (bi, hi, 0, 0)),
            pl.BlockSpec((1, 1, s, QK_PAD), lambda bi, hi: (bi, hi, 0, 0)),
            pl.BlockSpec((1, 1, VT_ROWS, s), lambda bi, hi: (bi, hi, 0, 0)),
        ],
        out_specs=pl.BlockSpec((1, s, V_DIM), lambda bi, hi: (bi, 0, hi)),
        out_shape=jax.ShapeDtypeStruct((b, s, h * V_DIM), BF16),
        scratch_shapes=[pltpu.VMEM((s, tq), F32), pltpu.VMEM((s, tq), F32), pltpu.VMEM((VT_ROWS, tq), F32)],
        compiler_params=pltpu.CompilerParams(
            dimension_semantics=("parallel", "parallel"), vmem_limit_bytes=VMEM_LIMIT_BIG),
        name="attention",
    )(qT, k, vT)


def _merge_kernel(x_ref, lng_ref, lnb_ref, a_ref, y_ref, gc_ref, bc_ref, wa_ref, wc_ref, g1_ref, b1_ref,
                  w1f_ref, w2f_ref, o_ref, w1b_ref, w2b_ref):
    _round_slabs([w1f_ref, w2f_ref], [w1b_ref, w2b_ref])
    x0 = _layer_norm(x_ref[...], lng_ref[...], lnb_ref[...])
    yn = _layer_norm(y_ref[...].reshape(y_ref.shape[0], CONV_CH), gc_ref[...], bc_ref[...])
    conv_out = (yn * jax.nn.sigmoid(yn)).astype(BF16)
    mix = (jnp.dot(a_ref[...], wa_ref[...], preferred_element_type=F32)
           + jnp.dot(conv_out, wc_ref[...], preferred_element_type=F32))
    o_ref[...] = _layer_norm(ALPHA * x0 + mix, g1_ref[...], b1_ref[...])


def _merge(x2d, ln_g, ln_b, attn2d, y2d, gc, bc, w_out_bf16, g1, b1, w1f, w2f, tm=512):
    t, d = x2d.shape
    assert t % tm == 0 and w_out_bf16.shape == (MLA_WIDTH + CONV_CH, d) and MLA_WIDTH == CONV_CH
    steps = t // tm
    w_specs, w_shapes = _slab_specs([w1f, w2f], steps, lambda i: i)
    row = lambda i: (i, 0)
    w_out_rows = lambda part: pl.BlockSpec((MLA_WIDTH, d), lambda i: (part, 0), pipeline_mode=pl.Buffered(1))
    return pl.pallas_call(
        _merge_kernel,
        grid=(steps,),
        in_specs=[
            pl.BlockSpec((tm, d), row), _const_spec((1, d)), _const_spec((1, d)),
            pl.BlockSpec((tm, MLA_WIDTH), row), pl.BlockSpec((tm, SUB, LANE), lambda i: (i, 0, 0)),
            _const_spec((1, CONV_CH)), _const_spec((1, CONV_CH)),
            w_out_rows(0), w_out_rows(1), _const_spec((1, d)), _const_spec((1, d)),
        ] + w_specs,
        out_specs=[pl.BlockSpec((tm, d), row)] + w_specs,
        out_shape=[jax.ShapeDtypeStruct((t, d), F32)] + w_shapes,
        compiler_params=pltpu.CompilerParams(
            dimension_semantics=("parallel",), vmem_limit_bytes=VMEM_LIMIT),
        name="merge",
    )(x2d, ln_g.reshape(1, d), ln_b.reshape(1, d), attn2d, y2d, gc.reshape(1, CONV_CH), bc.reshape(1, CONV_CH),
      w_out_bf16, w_out_bf16, g1.reshape(1, d), b1.reshape(1, d), w1f, w2f)


def _ffn_kernel(x_ref, w1_ref, w2_ref, g_ref, b_ref, o_ref, xb_ref):
    j = pl.program_id(1)

    @pl.when(j == 0)
    def _():
        xb_ref[...] = x_ref[...].astype(BF16)
        o_ref[...] = jnp.zeros(o_ref.shape, F32)

    h = jnp.dot(xb_ref[...], w1_ref[...], preferred_element_type=F32)
    h = jnp.maximum(h, 0.0)
    o_ref[...] += jnp.dot((h * h).astype(BF16), w2_ref[...], preferred_element_type=F32)

    @pl.when(j == pl.num_programs(1) - 1)
    def _():
        o_ref[...] = _layer_norm(ALPHA * x_ref[...] + o_ref[...], g_ref[...], b_ref[...])


def _ffn(x1, w1, w2, g2, b2, tm=512, tf=2048):
    t, d = x1.shape
    f = w1.shape[1]
    assert t % tm == 0 and f % tf == 0
    return pl.pallas_call(
        _ffn_kernel,
        grid=(t // tm, f // tf),
        in_specs=[
            pl.BlockSpec((tm, d), lambda i, j: (i, 0)),
            pl.BlockSpec((d, tf), lambda i, j: (0, j)),
            pl.BlockSpec((tf, d), lambda i, j: (j, 0)),
            _const_spec((1, d)), _const_spec((1, d)),
        ],
        out_specs=pl.BlockSpec((tm, d), lambda i, j: (i, 0)),
        out_shape=jax.ShapeDtypeStruct((t, d), F32),
        scratch_shapes=[pltpu.VMEM((tm, d), BF16)],
        compiler_params=pltpu.CompilerParams(
            dimension_semantics=("parallel", "arbitrary"), vmem_limit_bytes=VMEM_LIMIT_BIG),
        name="ffn",
    )(x1, w1, w2, g2.reshape(1, d), b2.reshape(1, d))


def kernel(x, positions, ln_in_g, ln_in_b, w_in, g_cq, w_uq, g_ckv, w_uk, w_uv, conv_w, conv_b,
           g_conv_ln, b_conv_ln, w_out, g_ln1, b_ln1, w_ff1, w_ff2, g_ln2, b_ln2):
    b, s, d = x.shape
    cosT, sinT, w_in0, w_uq0, wuk, w_uv0 = _rope_tables(positions, [w_in[0], w_uq[0], w_uk[0], w_uv[0]])
    o_kr = Q_RANK + KV_RANK
    wc = w_in0[:, :o_kr]
    wkrT = w_in0[:, o_kr:o_kr + ROPE].T
    wconv = w_in0[:, o_kr + ROPE:]
    wuqT = w_uq0.T
    wuvT = w_uv0.T

    qT, k, vT, u = _in_proj(x, ln_in_g, ln_in_b, wc, wkrT, wconv, g_cq[0], g_ckv[0],
                            wuqT, wuk, wuvT, cosT, sinT)
    y, w_out0 = _conv_module(u, conv_w[0], conv_b[0], w_out[0])
    attn = _attention(qT, k, vT)
    x2d = x.reshape(b * s, d)
    x1, w1, w2 = _merge(x2d, ln_in_g, ln_in_b, attn.reshape(b * s, MLA_WIDTH), y.reshape(b * s, SUB, LANE),
                        g_conv_ln[0], b_conv_ln[0], w_out0, g_ln1[0], b_ln1[0], w_ff1[0], w_ff2[0])
    out = _ffn(x1, w1, w2, g_ln2[0], b_ln2[0])
    return out.reshape(b, s, d)
```

```python
import functools
import math

import jax
import jax.numpy as jnp
from jax import lax
from jax.experimental import pallas as pl
from jax.experimental.pallas import tpu as pltpu

F32 = jnp.float32
BF16 = jnp.bfloat16

D_MODEL = 2048
HEADS = 8
NOPE = 128
ROPE = 64
HALF = ROPE // 2
QK = NOPE + ROPE
QK_PAD = 256
V_DIM = 128
VT_ROWS = V_DIM + 16
Q_RANK = 512
KV_RANK = 512
MLA_WIDTH = HEADS * V_DIM
CONV_CH = D_MODEL - MLA_WIDTH
CONV_W = 31
CONV_PAD = CONV_W // 2
HALO = 16
D_FF = 4 * D_MODEL
ROPE_BASE = 10000.0
LN_EPS = 1e-5
RMS_EPS = 1e-6
ALPHA = 2.0 ** 0.25
Q_SCALE = (QK ** -0.5) * math.log2(math.e)

SUB = 8
LANE = 128
VMEM_LIMIT = 56 * 1024 * 1024
VMEM_LIMIT_BIG = 60 * 1024 * 1024

NT_DIMS = (((1,), (1,)), ((), ()))


def _const_spec(shape):
    nd = len(shape)
    return pl.BlockSpec(shape, lambda *_: (0,) * nd, pipeline_mode=pl.Buffered(1))


def _layer_norm(x, g, b):
    mu = jnp.mean(x, axis=-1, keepdims=True)
    xc = x - mu
    var = jnp.mean(xc * xc, axis=-1, keepdims=True)
    return xc * lax.rsqrt(var + LN_EPS) * g + b


def _rope_kernel(pos_ref, invf_ref, cos_ref, sin_ref):
    ang = invf_ref[...] * pos_ref[0].astype(F32)
    cos_ref[0] = jnp.cos(ang)
    sin_ref[0] = jnp.sin(ang)


def _rope_tables(positions, ts=2048):
    b, s = positions.shape
    assert s % ts == 0
    inv_freq = ROPE_BASE ** (-jnp.arange(HALF, dtype=F32) * (2.0 / ROPE))
    out = jax.ShapeDtypeStruct((b, HALF, s), F32)
    return pl.pallas_call(
        _rope_kernel,
        grid=(b, s // ts),
        in_specs=[pl.BlockSpec((1, 1, ts), lambda i, j: (i, 0, j)),
                  pl.BlockSpec((HALF, 1), lambda i, j: (0, 0))],
        out_specs=[pl.BlockSpec((1, HALF, ts), lambda i, j: (i, 0, j))] * 2,
        out_shape=[out, out],
        name="rope_tables",
    )(positions.reshape(b, 1, s), inv_freq.reshape(HALF, 1))


def _inproj_kernel(x_ref, lng_ref, lnb_ref, wc_ref, wkrT_ref, wconv_ref, gcq_ref, gckv_ref,
                   wuqT_ref, wuk_ref, wuvT_ref, cos_ref, sin_ref,
                   qT_ref, k_ref, vT_ref, u_ref):
    tm = x_ref.shape[1]
    xb = _layer_norm(x_ref[0], lng_ref[...], lnb_ref[...]).astype(BF16)
    cos = cos_ref[0]
    sin = sin_ref[0]

    ag = lax.dot_general(xb, wconv_ref[...], NT_DIMS, preferred_element_type=F32)
    u = ag[:, :CONV_CH] * jax.nn.sigmoid(ag[:, CONV_CH:])
    u_ref[0] = u.reshape(tm, SUB, LANE)

    c = lax.dot_general(xb, wc_ref[...], NT_DIMS, preferred_element_type=F32)
    c_q = c[:, :Q_RANK]
    c_kv = c[:, Q_RANK:]
    rq = lax.rsqrt(jnp.mean(c_q * c_q, axis=-1, keepdims=True) + RMS_EPS) * Q_SCALE
    cqn = (c_q * rq * gcq_ref[...]).astype(BF16)
    rkv = lax.rsqrt(jnp.mean(c_kv * c_kv, axis=-1, keepdims=True) + RMS_EPS)
    ckvn = (c_kv * rkv * gckv_ref[...]).astype(BF16)

    qT = lax.dot_general(wuqT_ref[...], cqn, NT_DIMS, preferred_element_type=F32)
    zeros_q = jnp.zeros((QK_PAD - QK, tm), BF16)
    for h in range(HEADS):
        base = h * QK
        r1 = qT[base + NOPE:base + NOPE + HALF]
        r2 = qT[base + NOPE + HALF:base + QK]
        qT_ref[0, h, 0:NOPE, :] = qT[base:base + NOPE].astype(BF16)
        qT_ref[0, h, NOPE:NOPE + HALF, :] = (r1 * cos - r2 * sin).astype(BF16)
        qT_ref[0, h, NOPE + HALF:QK, :] = (r2 * cos + r1 * sin).astype(BF16)
        qT_ref[0, h, QK:QK_PAD, :] = zeros_q

    krT = lax.dot_general(wkrT_ref[...], xb, NT_DIMS, preferred_element_type=F32)
    k1 = krT[:HALF]
    k2 = krT[HALF:]
    kr_fullT = jnp.concatenate(
        [k1 * cos - k2 * sin, k2 * cos + k1 * sin, jnp.zeros((QK_PAD - QK, tm), F32)], axis=0)
    kr = kr_fullT.T.astype(BF16)

    kn = jnp.dot(ckvn, wuk_ref[...], preferred_element_type=F32)
    vT = lax.dot_general(wuvT_ref[...], ckvn, NT_DIMS, preferred_element_type=F32)
    row = lax.broadcasted_iota(jnp.int32, (VT_ROWS - V_DIM, tm), 0)
    ones_row = jnp.where(row == 0, 1.0, 0.0).astype(BF16)
    for h in range(HEADS):
        k_ref[0, h, :, 0:NOPE] = kn[:, h * NOPE:(h + 1) * NOPE].astype(BF16)
        k_ref[0, h, :, NOPE:QK_PAD] = kr
        vT_ref[0, h, 0:V_DIM, :] = vT[h * V_DIM:(h + 1) * V_DIM].astype(BF16)
        vT_ref[0, h, V_DIM:VT_ROWS, :] = ones_row


def _in_proj(x, ln_g, ln_b, wc, wkrT, wconv, g_cq, g_ckv, wuqT, wuk, wuvT, cosT, sinT, tm=512):
    b, s, d = x.shape
    assert s % tm == 0 and d == D_MODEL and wconv.shape == (d, 2 * CONV_CH)
    grid = (b, s // tm)
    out_shape = [
        jax.ShapeDtypeStruct((b, HEADS, QK_PAD, s), BF16),
        jax.ShapeDtypeStruct((b, HEADS, s, QK_PAD), BF16),
        jax.ShapeDtypeStruct((b, HEADS, VT_ROWS, s), BF16),
        jax.ShapeDtypeStruct((b, s, SUB, LANE), F32),
    ]
    in_specs = [
        pl.BlockSpec((1, tm, d), lambda i, j: (i, j, 0)),
        _const_spec((1, d)), _const_spec((1, d)),
        _const_spec(wc.shape), _const_spec(wkrT.shape), _const_spec(wconv.shape),
        _const_spec((1, Q_RANK)), _const_spec((1, KV_RANK)),
        _const_spec(wuqT.shape), _const_spec(wuk.shape), _const_spec(wuvT.shape),
        pl.BlockSpec((1, HALF, tm), lambda i, j: (i, 0, j)),
        pl.BlockSpec((1, HALF, tm), lambda i, j: (i, 0, j)),
    ]
    out_specs = [
        pl.BlockSpec((1, HEADS, QK_PAD, tm), lambda i, j: (i, 0, 0, j)),
        pl.BlockSpec((1, HEADS, tm, QK_PAD), lambda i, j: (i, 0, j, 0)),
        pl.BlockSpec((1, HEADS, VT_ROWS, tm), lambda i, j: (i, 0, 0, j)),
        pl.BlockSpec((1, tm, SUB, LANE), lambda i, j: (i, j, 0, 0)),
    ]
    return pl.pallas_call(
        _inproj_kernel, grid=grid, in_specs=in_specs, out_specs=out_specs, out_shape=out_shape,
        compiler_params=pltpu.CompilerParams(
            dimension_semantics=("parallel", "parallel"), vmem_limit_bytes=VMEM_LIMIT),
        name="in_proj",
    )(x, ln_g.reshape(1, d), ln_b.reshape(1, d), wc, wkrT, wconv,
      g_cq.reshape(1, Q_RANK), g_ckv.reshape(1, KV_RANK), wuqT, wuk, wuvT, cosT, sinT)


CONV_TB = 16


def _conv_kernel(u_ref, prev_ref, next_ref, w_ref, b_ref, y_ref, ext_ref):
    tc = u_ref.shape[1]
    i = pl.program_id(1)
    last = pl.num_programs(1) - 1
    ext_ref[0:HALO] = jnp.where(i > 0, prev_ref[0], 0.0)
    ext_ref[HALO:HALO + tc] = u_ref[0]
    ext_ref[HALO + tc:HALO + tc + HALO] = jnp.where(i < last, next_ref[0], 0.0)
    off = HALO - CONV_PAD
    bias = b_ref[...]

    for t0 in range(0, tc, CONV_TB):
        acc = jnp.zeros((CONV_TB, SUB, LANE), F32)
        for k in range(CONV_W):
            acc = acc + w_ref[k] * ext_ref[t0 + off + k:t0 + off + k + CONV_TB]
        y_ref[0, t0:t0 + CONV_TB] = acc + bias


def _conv_module(u4, conv_w, conv_b, tc=512):
    b, s = u4.shape[:2]
    assert s % tc == 0 and tc % CONV_TB == 0 and tc % HALO == 0 and HALO >= CONV_PAD
    nh = tc // HALO
    n_halo_blocks = s // HALO
    return pl.pallas_call(
        _conv_kernel,
        grid=(b, s // tc),
        in_specs=[
            pl.BlockSpec((1, tc, SUB, LANE), lambda i, j: (i, j, 0, 0)),
            pl.BlockSpec((1, HALO, SUB, LANE), lambda i, j: (i, jnp.maximum(j * nh - 1, 0), 0, 0)),
            pl.BlockSpec((1, HALO, SUB, LANE),
                         lambda i, j: (i, jnp.minimum((j + 1) * nh, n_halo_blocks - 1), 0, 0)),
            _const_spec((CONV_W, SUB, LANE)), _const_spec((SUB, LANE)),
        ],
        out_specs=pl.BlockSpec((1, tc, SUB, LANE), lambda i, j: (i, j, 0, 0)),
        out_shape=jax.ShapeDtypeStruct((b, s, SUB, LANE), F32),
        scratch_shapes=[pltpu.VMEM((tc + 2 * HALO, SUB, LANE), F32)],
        compiler_params=pltpu.CompilerParams(dimension_semantics=("parallel", "parallel")),
        name="conv_module",
    )(u4, u4, u4, conv_w.reshape(CONV_W, SUB, LANE), conv_b.reshape(SUB, LANE))


def _attn_kernel(qT_ref, k_ref, vT_ref, o_ref, sa_ref, sb_ref, acc_ref, *, tq, tk, ts):
    s_len = k_ref.shape[2]
    nq = s_len // tq
    nchunk = s_len // tk
    neg = jnp.full((SUB, tq), -jnp.inf, F32)

    def q_block(q):
        return qT_ref[0, 0, :, pl.ds(pl.multiple_of(q * tq, tq), tq)]

    def score_chunk(start, qT, s_ref, mpart):
        s = jnp.dot(k_ref[0, 0, pl.ds(start, ts), :], qT, preferred_element_type=F32)
        s_ref[pl.ds(start, ts), :] = s
        return jnp.maximum(mpart, jnp.max(s.reshape(ts // SUB, SUB, tq), axis=0))

    def weight_chunk(start, s_ref, m):
        p = jnp.exp2(s_ref[pl.ds(start, ts), :] - m).astype(BF16)
        acc_ref[...] += jnp.dot(vT_ref[0, 0, :, pl.ds(start, ts)], p, preferred_element_type=F32)

    def finish(q):
        o = acc_ref[0:V_DIM, :] / acc_ref[V_DIM:V_DIM + 1, :]
        o_ref[0, pl.ds(pl.multiple_of(q * tq, tq), tq), :] = o.T.astype(o_ref.dtype)

    def sweep(q_next, s_next_ref, q_cur, s_cur_ref, mpart_cur):
        qT = None if q_next is None else q_block(q_next)
        if q_cur is not None:
            m = jnp.max(mpart_cur, axis=0, keepdims=True)
            acc_ref[...] = jnp.zeros(acc_ref.shape, F32)

        def body(c, mpart):
            for j in range(tk // ts):
                start = pl.multiple_of(c * tk + j * ts, ts)
                if q_cur is not None:
                    weight_chunk(start, s_cur_ref, m)
                if q_next is not None:
                    mpart = score_chunk(start, qT, s_next_ref, mpart)
            return mpart

        mpart_next = lax.fori_loop(0, nchunk, body, neg)
        if q_cur is not None:
            finish(q_cur)
        return mpart_next

    mpart_a = sweep(0, sa_ref, None, None, None)

    def pair(i, mpart_a):
        q = 2 * i
        mpart_b = sweep(q + 1, sb_ref, q, sa_ref, mpart_a)
        return sweep(q + 2, sa_ref, q + 1, sb_ref, mpart_b)

    mpart_a = lax.fori_loop(0, nq // 2 - 1, pair, mpart_a)
    mpart_b = sweep(nq - 1, sb_ref, nq - 2, sa_ref, mpart_a)
    sweep(None, None, nq - 1, sb_ref, mpart_b)


def _attention(qT, k, vT, tq=512, tk=8192, ts=256):
    b, h, _, s = qT.shape
    assert s % tk == 0 and tk % ts == 0 and s % (2 * tq) == 0 and s // tq >= 4
    return pl.pallas_call(
        functools.partial(_attn_kernel, tq=tq, tk=tk, ts=ts),
        grid=(b, h),
        in_specs=[
            pl.BlockSpec((1, 1, QK_PAD, s), lambda bi, hi: (bi, hi, 0, 0)),
            pl.BlockSpec((1, 1, s, QK_PAD), lambda bi, hi: (bi, hi, 0, 0)),
            pl.BlockSpec((1, 1, VT_ROWS, s), lambda bi, hi: (bi, hi, 0, 0)),
        ],
        out_specs=pl.BlockSpec((1, s, V_DIM), lambda bi, hi: (bi, 0, hi)),
        out_shape=jax.ShapeDtypeStruct((b, s, h * V_DIM), BF16),
        scratch_shapes=[pltpu.VMEM((s, tq), F32), pltpu.VMEM((s, tq), F32), pltpu.VMEM((VT_ROWS, tq), F32)],
        compiler_params=pltpu.CompilerParams(
            dimension_semantics=("parallel", "parallel"), vmem_limit_bytes=VMEM_LIMIT_BIG),
        name="attention",
    )(qT, k, vT)


def _merge_kernel(x_ref, lng_ref, lnb_ref, a_ref, y_ref, gc_ref, bc_ref, wa_ref, wc_ref, g1_ref, b1_ref,
                  w1f_ref, w2f_ref, o_ref, w1b_ref, w2b_ref):
    w1b_ref[...] = w1f_ref[...].astype(BF16)
    w2b_ref[...] = w2f_ref[...].astype(BF16)
    x0 = _layer_norm(x_ref[...], lng_ref[...], lnb_ref[...])
    yn = _layer_norm(y_ref[...].reshape(y_ref.shape[0], CONV_CH), gc_ref[...], bc_ref[...])
    conv_out = (yn * jax.nn.sigmoid(yn)).astype(BF16)
    mix = (jnp.dot(a_ref[...], wa_ref[...], preferred_element_type=F32)
           + jnp.dot(conv_out, wc_ref[...], preferred_element_type=F32))
    o_ref[...] = _layer_norm(ALPHA * x0 + mix, g1_ref[...], b1_ref[...])


def _merge(x2d, ln_g, ln_b, attn2d, y2d, gc, bc, wa, wc, g1, b1, w1f, w2f, tm=512):
    t, d = x2d.shape
    assert t % tm == 0
    steps = t // tm
    r1, r2 = w1f.shape[0] // steps, w2f.shape[0] // steps
    bf16_rows = 2 * SUB
    assert r1 * steps == w1f.shape[0] and r2 * steps == w2f.shape[0] and r1 % bf16_rows == 0 and r2 % bf16_rows == 0
    row = lambda i: (i, 0)
    return pl.pallas_call(
        _merge_kernel,
        grid=(steps,),
        in_specs=[
            pl.BlockSpec((tm, d), row), _const_spec((1, d)), _const_spec((1, d)),
            pl.BlockSpec((tm, MLA_WIDTH), row), pl.BlockSpec((tm, SUB, LANE), lambda i: (i, 0, 0)),
            _const_spec((1, CONV_CH)), _const_spec((1, CONV_CH)),
            _const_spec(wa.shape), _const_spec(wc.shape), _const_spec((1, d)), _const_spec((1, d)),
            pl.BlockSpec((r1, w1f.shape[1]), row), pl.BlockSpec((r2, w2f.shape[1]), row),
        ],
        out_specs=[pl.BlockSpec((tm, d), row),
                   pl.BlockSpec((r1, w1f.shape[1]), row), pl.BlockSpec((r2, w2f.shape[1]), row)],
        out_shape=[jax.ShapeDtypeStruct((t, d), F32),
                   jax.ShapeDtypeStruct(w1f.shape, BF16), jax.ShapeDtypeStruct(w2f.shape, BF16)],
        compiler_params=pltpu.CompilerParams(
            dimension_semantics=("parallel",), vmem_limit_bytes=VMEM_LIMIT),
        name="merge",
    )(x2d, ln_g.reshape(1, d), ln_b.reshape(1, d), attn2d, y2d, gc.reshape(1, CONV_CH), bc.reshape(1, CONV_CH),
      wa, wc, g1.reshape(1, d), b1.reshape(1, d), w1f, w2f)


def _ffn_kernel(x_ref, w1_ref, w2_ref, g_ref, b_ref, o_ref, xb_ref):
    j = pl.program_id(1)

    @pl.when(j == 0)
    def _():
        xb_ref[...] = x_ref[...].astype(BF16)
        o_ref[...] = jnp.zeros(o_ref.shape, F32)

    h = jnp.dot(xb_ref[...], w1_ref[...], preferred_element_type=F32)
    h = jnp.maximum(h, 0.0)
    o_ref[...] += jnp.dot((h * h).astype(BF16), w2_ref[...], preferred_element_type=F32)

    @pl.when(j == pl.num_programs(1) - 1)
    def _():
        o_ref[...] = _layer_norm(ALPHA * x_ref[...] + o_ref[...], g_ref[...], b_ref[...])


def _ffn(x1, w1, w2, g2, b2, tm=512, tf=2048):
    t, d = x1.shape
    f = w1.shape[1]
    assert t % tm == 0 and f % tf == 0
    return pl.pallas_call(
        _ffn_kernel,
        grid=(t // tm, f // tf),
        in_specs=[
            pl.BlockSpec((tm, d), lambda i, j: (i, 0)),
            pl.BlockSpec((d, tf), lambda i, j: (0, j)),
            pl.BlockSpec((tf, d), lambda i, j: (j, 0)),
            _const_spec((1, d)), _const_spec((1, d)),
        ],
        out_specs=pl.BlockSpec((tm, d), lambda i, j: (i, 0)),
        out_shape=jax.ShapeDtypeStruct((t, d), F32),
        scratch_shapes=[pltpu.VMEM((tm, d), BF16)],
        compiler_params=pltpu.CompilerParams(
            dimension_semantics=("parallel", "arbitrary"), vmem_limit_bytes=VMEM_LIMIT_BIG),
        name="ffn",
    )(x1, w1, w2, g2.reshape(1, d), b2.reshape(1, d))


def kernel(x, positions, ln_in_g, ln_in_b, w_in, g_cq, w_uq, g_ckv, w_uk, w_uv, conv_w, conv_b,
           g_conv_ln, b_conv_ln, w_out, g_ln1, b_ln1, w_ff1, w_ff2, g_ln2, b_ln2):
    b, s, d = x.shape
    w_inT = w_in[0].T.astype(BF16)
    o_kr = Q_RANK + KV_RANK
    wc = w_inT[:o_kr]
    wkrT = w_inT[o_kr:o_kr + ROPE]
    wconv = w_inT[o_kr + ROPE:]
    wuqT = w_uq[0].T.astype(BF16)
    wuk = w_uk[0].astype(BF16)
    wuvT = w_uv[0].T.astype(BF16)
    wa = w_out[0, :MLA_WIDTH].astype(BF16)
    wcv = w_out[0, MLA_WIDTH:].astype(BF16)

    cosT, sinT = _rope_tables(positions)
    qT, k, vT, u = _in_proj(x, ln_in_g, ln_in_b, wc, wkrT, wconv, g_cq[0], g_ckv[0],
                            wuqT, wuk, wuvT, cosT, sinT)
    y = _conv_module(u, conv_w[0], conv_b[0])
    attn = _attention(qT, k, vT)
    x2d = x.reshape(b * s, d)
    x1, w1, w2 = _merge(x2d, ln_in_g, ln_in_b, attn.reshape(b * s, MLA_WIDTH), y.reshape(b * s, SUB, LANE),
                        g_conv_ln[0], b_conv_ln[0], wa, wcv, g_ln1[0], b_ln1[0], w_ff1[0], w_ff2[0])
    out = _ffn(x1, w1, w2, g_ln2[0], b_ln2[0])
    return out.reshape(b, s, d)
```

```python
import functools
import math

import jax
import jax.numpy as jnp
from jax import lax
from jax.experimental import pallas as pl
from jax.experimental.pallas import tpu as pltpu

F32 = jnp.float32
BF16 = jnp.bfloat16

D_MODEL = 2048
HEADS = 8
NOPE = 128
ROPE = 64
HALF = ROPE // 2
QK = NOPE + ROPE
QK_PAD = 256
V_DIM = 128
VT_ROWS = V_DIM + 16
Q_RANK = 512
KV_RANK = 512
MLA_WIDTH = HEADS * V_DIM
CONV_CH = D_MODEL - MLA_WIDTH
CONV_W = 31
CONV_PAD = CONV_W // 2
HALO = 16
D_FF = 4 * D_MODEL
ROPE_BASE = 10000.0
LN_EPS = 1e-5
RMS_EPS = 1e-6
ALPHA = 2.0 ** 0.25
Q_SCALE = (QK ** -0.5) * math.log2(math.e)

SUB = 8
LANE = 128
VMEM_LIMIT = 56 * 1024 * 1024
VMEM_LIMIT_BIG = 60 * 1024 * 1024

NT_DIMS = (((1,), (1,)), ((), ()))


def _const_spec(shape):
    nd = len(shape)
    return pl.BlockSpec(shape, lambda *_: (0,) * nd, pipeline_mode=pl.Buffered(1))


def _layer_norm(x, g, b):
    mu = jnp.mean(x, axis=-1, keepdims=True)
    xc = x - mu
    var = jnp.mean(xc * xc, axis=-1, keepdims=True)
    return xc * lax.rsqrt(var + LN_EPS) * g + b


def _slab_specs(weights, steps, step_of):
    specs, shapes = [], []
    for w in weights:
        rows = w.shape[0] // steps
        assert rows * steps == w.shape[0] and rows % (2 * SUB) == 0
        specs.append(pl.BlockSpec((rows, w.shape[1]), lambda *g: (step_of(*g), 0)))
        shapes.append(jax.ShapeDtypeStruct(w.shape, BF16))
    return specs, shapes


def _round_slabs(f32_refs, bf16_refs):
    for src, dst in zip(f32_refs, bf16_refs):
        dst[...] = src[...].astype(BF16)


def _rope_kernel(pos_ref, invf_ref, cos_ref, sin_ref):
    ang = invf_ref[...] * pos_ref[0].astype(F32)
    cos_ref[0] = jnp.cos(ang)
    sin_ref[0] = jnp.sin(ang)


def _rope_tables(positions, ts=2048):
    b, s = positions.shape
    assert s % ts == 0
    inv_freq = ROPE_BASE ** (-jnp.arange(HALF, dtype=F32) * (2.0 / ROPE))
    out = jax.ShapeDtypeStruct((b, HALF, s), F32)
    return pl.pallas_call(
        _rope_kernel,
        grid=(b, s // ts),
        in_specs=[pl.BlockSpec((1, 1, ts), lambda i, j: (i, 0, j)),
                  pl.BlockSpec((HALF, 1), lambda i, j: (0, 0))],
        out_specs=[pl.BlockSpec((1, HALF, ts), lambda i, j: (i, 0, j))] * 2,
        out_shape=[out, out],
        name="rope_tables",
    )(positions.reshape(b, 1, s), inv_freq.reshape(HALF, 1))


def _inproj_kernel(x_ref, lng_ref, lnb_ref, wc_ref, wkrT_ref, wconv_ref, gcq_ref, gckv_ref,
                   wuqT_ref, wuk_ref, wuvT_ref, cos_ref, sin_ref,
                   qT_ref, k_ref, vT_ref, u_ref):
    tm = x_ref.shape[1]
    xb = _layer_norm(x_ref[0], lng_ref[...], lnb_ref[...]).astype(BF16)
    cos = cos_ref[0]
    sin = sin_ref[0]

    ag = lax.dot_general(xb, wconv_ref[...], NT_DIMS, preferred_element_type=F32)
    u = ag[:, :CONV_CH] * jax.nn.sigmoid(ag[:, CONV_CH:])
    u_ref[0] = u.reshape(tm, SUB, LANE)

    c = lax.dot_general(xb, wc_ref[...], NT_DIMS, preferred_element_type=F32)
    c_q = c[:, :Q_RANK]
    c_kv = c[:, Q_RANK:]
    rq = lax.rsqrt(jnp.mean(c_q * c_q, axis=-1, keepdims=True) + RMS_EPS) * Q_SCALE
    cqn = (c_q * rq * gcq_ref[...]).astype(BF16)
    rkv = lax.rsqrt(jnp.mean(c_kv * c_kv, axis=-1, keepdims=True) + RMS_EPS)
    ckvn = (c_kv * rkv * gckv_ref[...]).astype(BF16)

    qT = lax.dot_general(wuqT_ref[...], cqn, NT_DIMS, preferred_element_type=F32)
    zeros_q = jnp.zeros((QK_PAD - QK, tm), BF16)
    for h in range(HEADS):
        base = h * QK
        r1 = qT[base + NOPE:base + NOPE + HALF]
        r2 = qT[base + NOPE + HALF:base + QK]
        qT_ref[0, h, 0:NOPE, :] = qT[base:base + NOPE].astype(BF16)
        qT_ref[0, h, NOPE:NOPE + HALF, :] = (r1 * cos - r2 * sin).astype(BF16)
        qT_ref[0, h, NOPE + HALF:QK, :] = (r2 * cos + r1 * sin).astype(BF16)
        qT_ref[0, h, QK:QK_PAD, :] = zeros_q

    krT = lax.dot_general(wkrT_ref[...], xb, NT_DIMS, preferred_element_type=F32)
    k1 = krT[:HALF]
    k2 = krT[HALF:]
    kr_fullT = jnp.concatenate(
        [k1 * cos - k2 * sin, k2 * cos + k1 * sin, jnp.zeros((QK_PAD - QK, tm), F32)], axis=0)
    kr = kr_fullT.T.astype(BF16)

    kn = jnp.dot(ckvn, wuk_ref[...], preferred_element_type=F32)
    vT = lax.dot_general(wuvT_ref[...], ckvn, NT_DIMS, preferred_element_type=F32)
    row = lax.broadcasted_iota(jnp.int32, (VT_ROWS - V_DIM, tm), 0)
    ones_row = jnp.where(row == 0, 1.0, 0.0).astype(BF16)
    for h in range(HEADS):
        k_ref[0, h, :, 0:NOPE] = kn[:, h * NOPE:(h + 1) * NOPE].astype(BF16)
        k_ref[0, h, :, NOPE:QK_PAD] = kr
        vT_ref[0, h, 0:V_DIM, :] = vT[h * V_DIM:(h + 1) * V_DIM].astype(BF16)
        vT_ref[0, h, V_DIM:VT_ROWS, :] = ones_row


def _in_proj(x, ln_g, ln_b, wc, wkrT, wconv, g_cq, g_ckv, wuqT, wuk, wuvT, cosT, sinT, tm=512):
    b, s, d = x.shape
    assert s % tm == 0 and d == D_MODEL and wconv.shape == (d, 2 * CONV_CH)
    grid = (b, s // tm)
    out_shape = [
        jax.ShapeDtypeStruct((b, HEADS, QK_PAD, s), BF16),
        jax.ShapeDtypeStruct((b, HEADS, s, QK_PAD), BF16),
        jax.ShapeDtypeStruct((b, HEADS, VT_ROWS, s), BF16),
        jax.ShapeDtypeStruct((b, s, SUB, LANE), F32),
    ]
    in_specs = [
        pl.BlockSpec((1, tm, d), lambda i, j: (i, j, 0)),
        _const_spec((1, d)), _const_spec((1, d)),
        _const_spec(wc.shape), _const_spec(wkrT.shape), _const_spec(wconv.shape),
        _const_spec((1, Q_RANK)), _const_spec((1, KV_RANK)),
        _const_spec(wuqT.shape), _const_spec(wuk.shape), _const_spec(wuvT.shape),
        pl.BlockSpec((1, HALF, tm), lambda i, j: (i, 0, j)),
        pl.BlockSpec((1, HALF, tm), lambda i, j: (i, 0, j)),
    ]
    out_specs = [
        pl.BlockSpec((1, HEADS, QK_PAD, tm), lambda i, j: (i, 0, 0, j)),
        pl.BlockSpec((1, HEADS, tm, QK_PAD), lambda i, j: (i, 0, j, 0)),
        pl.BlockSpec((1, HEADS, VT_ROWS, tm), lambda i, j: (i, 0, 0, j)),
        pl.BlockSpec((1, tm, SUB, LANE), lambda i, j: (i, j, 0, 0)),
    ]
    return pl.pallas_call(
        _inproj_kernel, grid=grid, in_specs=in_specs, out_specs=out_specs, out_shape=out_shape,
        compiler_params=pltpu.CompilerParams(
            dimension_semantics=("parallel", "parallel"), vmem_limit_bytes=VMEM_LIMIT),
        name="in_proj",
    )(x, ln_g.reshape(1, d), ln_b.reshape(1, d), wc, wkrT, wconv,
      g_cq.reshape(1, Q_RANK), g_ckv.reshape(1, KV_RANK), wuqT, wuk, wuvT, cosT, sinT)


CONV_TB = 16


def _conv_kernel(u_ref, prev_ref, next_ref, w_ref, b_ref, wo_f32_ref, y_ref, wo_bf16_ref, ext_ref):
    _round_slabs([wo_f32_ref], [wo_bf16_ref])
    tc = u_ref.shape[1]
    i = pl.program_id(1)
    last = pl.num_programs(1) - 1
    ext_ref[0:HALO] = jnp.where(i > 0, prev_ref[0], 0.0)
    ext_ref[HALO:HALO + tc] = u_ref[0]
    ext_ref[HALO + tc:HALO + tc + HALO] = jnp.where(i < last, next_ref[0], 0.0)
    off = HALO - CONV_PAD
    bias = b_ref[...]

    for t0 in range(0, tc, CONV_TB):
        acc = jnp.zeros((CONV_TB, SUB, LANE), F32)
        for k in range(CONV_W):
            acc = acc + w_ref[k] * ext_ref[t0 + off + k:t0 + off + k + CONV_TB]
        y_ref[0, t0:t0 + CONV_TB] = acc + bias


def _conv_module(u4, conv_w, conv_b, w_out_f32, tc=512):
    b, s = u4.shape[:2]
    assert s % tc == 0 and tc % CONV_TB == 0 and tc % HALO == 0 and HALO >= CONV_PAD
    nh = tc // HALO
    nj = s // tc
    n_halo_blocks = s // HALO
    w_specs, w_shapes = _slab_specs([w_out_f32], b * nj, lambda i, j: i * nj + j)
    return pl.pallas_call(
        _conv_kernel,
        grid=(b, nj),
        in_specs=[
            pl.BlockSpec((1, tc, SUB, LANE), lambda i, j: (i, j, 0, 0)),
            pl.BlockSpec((1, HALO, SUB, LANE), lambda i, j: (i, jnp.maximum(j * nh - 1, 0), 0, 0)),
            pl.BlockSpec((1, HALO, SUB, LANE),
                         lambda i, j: (i, jnp.minimum((j + 1) * nh, n_halo_blocks - 1), 0, 0)),
            _const_spec((CONV_W, SUB, LANE)), _const_spec((SUB, LANE)),
        ] + w_specs,
        out_specs=[pl.BlockSpec((1, tc, SUB, LANE), lambda i, j: (i, j, 0, 0))] + w_specs,
        out_shape=[jax.ShapeDtypeStruct((b, s, SUB, LANE), F32)] + w_shapes,
        scratch_shapes=[pltpu.VMEM((tc + 2 * HALO, SUB, LANE), F32)],
        compiler_params=pltpu.CompilerParams(dimension_semantics=("parallel", "parallel")),
        name="conv_module",
    )(u4, u4, u4, conv_w.reshape(CONV_W, SUB, LANE), conv_b.reshape(SUB, LANE), w_out_f32)


def _attn_kernel(qT_ref, k_ref, vT_ref, o_ref, sa_ref, sb_ref, acc_ref, *, tq, tk, ts):
    s_len = k_ref.shape[2]
    nq = s_len // tq
    nchunk = s_len // tk
    neg = jnp.full((SUB, tq), -jnp.inf, F32)

    def q_block(q):
        return qT_ref[0, 0, :, pl.ds(pl.multiple_of(q * tq, tq), tq)]

    def score_chunk(start, qT, s_ref, mpart):
        s = jnp.dot(k_ref[0, 0, pl.ds(start, ts), :], qT, preferred_element_type=F32)
        s_ref[pl.ds(start, ts), :] = s
        return jnp.maximum(mpart, jnp.max(s.reshape(ts // SUB, SUB, tq), axis=0))

    def weight_chunk(start, s_ref, m):
        p = jnp.exp2(s_ref[pl.ds(start, ts), :] - m).astype(BF16)
        acc_ref[...] += jnp.dot(vT_ref[0, 0, :, pl.ds(start, ts)], p, preferred_element_type=F32)

    def finish(q):
        o = acc_ref[0:V_DIM, :] / acc_ref[V_DIM:V_DIM + 1, :]
        o_ref[0, pl.ds(pl.multiple_of(q * tq, tq), tq), :] = o.T.astype(o_ref.dtype)

    def sweep(q_next, s_next_ref, q_cur, s_cur_ref, mpart_cur):
        qT = None if q_next is None else q_block(q_next)
        if q_cur is not None:
            m = jnp.max(mpart_cur, axis=0, keepdims=True)
            acc_ref[...] = jnp.zeros(acc_ref.shape, F32)

        def body(c, mpart):
            for j in range(tk // ts):
                start = pl.multiple_of(c * tk + j * ts, ts)
                if q_cur is not None:
                    weight_chunk(start, s_cur_ref, m)
                if q_next is not None:
                    mpart = score_chunk(start, qT, s_next_ref, mpart)
            return mpart

        mpart_next = lax.fori_loop(0, nchunk, body, neg)
        if q_cur is not None:
            finish(q_cur)
        return mpart_next

    mpart_a = sweep(0, sa_ref, None, None, None)

    def pair(i, mpart_a):
        q = 2 * i
        mpart_b = sweep(q + 1, sb_ref, q, sa_ref, mpart_a)
        return sweep(q + 2, sa_ref, q + 1, sb_ref, mpart_b)

    mpart_a = lax.fori_loop(0, nq // 2 - 1, pair, mpart_a)
    mpart_b = sweep(nq - 1, sb_ref, nq - 2, sa_ref, mpart_a)
    sweep(None, None, nq - 1, sb_ref, mpart_b)


def _attention(qT, k, vT, tq=512, tk=8192, ts=256):
    b, h, _, s = qT.shape
    assert s % tk == 0 and tk % ts == 0 and s % (2 * tq) == 0 and s // tq >= 4
    return pl.pallas_call(
        functools.partial(_attn_kernel, tq=tq, tk=tk, ts=ts),
        grid=(b, h),
        in_specs=[
            pl.BlockSpec((1, 1, QK_PAD, s), lambda bi, hi: (bi, hi, 0, 0)),
            pl.BlockSpec((1, 1, s, QK_PAD), lambda bi, hi: (bi, hi, 0, 0)),
            pl.BlockSpec((1, 1, VT_ROWS, s), lambda bi, hi: (bi, hi, 0, 0)),
        ],
        out_specs=pl.BlockSpec((1, s, V_DIM), lambda bi, hi: (bi, 0, hi)),
        out_shape=jax.ShapeDtypeStruct((b, s, h * V_DIM), BF16),
        scratch_shapes=[pltpu.VMEM((s, tq), F32), pltpu.VMEM((s, tq), F32), pltpu.VMEM((VT_ROWS, tq), F32)],
        compiler_params=pltpu.CompilerParams(
            dimension_semantics=("parallel", "parallel"), vmem_limit_bytes=VMEM_LIMIT_BIG),
        name="attention",
    )(qT, k, vT)


def _merge_kernel(x_ref, lng_ref, lnb_ref, a_ref, y_ref, gc_ref, bc_ref, wa_ref, wc_ref, g1_ref, b1_ref,
                  w1f_ref, w2f_ref, o_ref, w1b_ref, w2b_ref):
    w1b_ref[...] = w1f_ref[...].astype(BF16)
    w2b_ref[...] = w2f_ref[...].astype(BF16)
    x0 = _layer_norm(x_ref[...], lng_ref[...], lnb_ref[...])
    yn = _layer_norm(y_ref[...].reshape(y_ref.shape[0], CONV_CH), gc_ref[...], bc_ref[...])
    conv_out = (yn * jax.nn.sigmoid(yn)).astype(BF16)
    mix = (jnp.dot(a_ref[...], wa_ref[...], preferred_element_type=F32)
           + jnp.dot(conv_out, wc_ref[...], preferred_element_type=F32))
    o_ref[...] = _layer_norm(ALPHA * x0 + mix, g1_ref[...], b1_ref[...])


def _merge(x2d, ln_g, ln_b, attn2d, y2d, gc, bc, w_out_bf16, g1, b1, w1f, w2f, tm=512):
    t, d = x2d.shape
    assert t % tm == 0 and w_out_bf16.shape == (MLA_WIDTH + CONV_CH, d) and MLA_WIDTH == CONV_CH
    w_out_rows = lambda part: pl.BlockSpec((MLA_WIDTH, d), lambda i: (part, 0), pipeline_mode=pl.Buffered(1))
    steps = t // tm
    r1, r2 = w1f.shape[0] // steps, w2f.shape[0] // steps
    bf16_rows = 2 * SUB
    assert r1 * steps == w1f.shape[0] and r2 * steps == w2f.shape[0] and r1 % bf16_rows == 0 and r2 % bf16_rows == 0
    row = lambda i: (i, 0)
    return pl.pallas_call(
        _merge_kernel,
        grid=(steps,),
        in_specs=[
            pl.BlockSpec((tm, d), row), _const_spec((1, d)), _const_spec((1, d)),
            pl.BlockSpec((tm, MLA_WIDTH), row), pl.BlockSpec((tm, SUB, LANE), lambda i: (i, 0, 0)),
            _const_spec((1, CONV_CH)), _const_spec((1, CONV_CH)),
            w_out_rows(0), w_out_rows(1), _const_spec((1, d)), _const_spec((1, d)),
            pl.BlockSpec((r1, w1f.shape[1]), row), pl.BlockSpec((r2, w2f.shape[1]), row),
        ],
        out_specs=[pl.BlockSpec((tm, d), row),
                   pl.BlockSpec((r1, w1f.shape[1]), row), pl.BlockSpec((r2, w2f.shape[1]), row)],
        out_shape=[jax.ShapeDtypeStruct((t, d), F32),
                   jax.ShapeDtypeStruct(w1f.shape, BF16), jax.ShapeDtypeStruct(w2f.shape, BF16)],
        compiler_params=pltpu.CompilerParams(
            dimension_semantics=("parallel",), vmem_limit_bytes=VMEM_LIMIT),
        name="merge",
    )(x2d, ln_g.reshape(1, d), ln_b.reshape(1, d), attn2d, y2d, gc.reshape(1, CONV_CH), bc.reshape(1, CONV_CH),
      w_out_bf16, w_out_bf16, g1.reshape(1, d), b1.reshape(1, d), w1f, w2f)


def _ffn_kernel(x_ref, w1_ref, w2_ref, g_ref, b_ref, o_ref, xb_ref):
    j = pl.program_id(1)

    @pl.when(j == 0)
    def _():
        xb_ref[...] = x_ref[...].astype(BF16)
        o_ref[...] = jnp.zeros(o_ref.shape, F32)

    h = jnp.dot(xb_ref[...], w1_ref[...], preferred_element_type=F32)
    h = jnp.maximum(h, 0.0)
    o_ref[...] += jnp.dot((h * h).astype(BF16), w2_ref[...], preferred_element_type=F32)

    @pl.when(j == pl.num_programs(1) - 1)
    def _():
        o_ref[...] = _layer_norm(ALPHA * x_ref[...] + o_ref[...], g_ref[...], b_ref[...])


def _ffn(x1, w1, w2, g2, b2, tm=512, tf=2048):
    t, d = x1.shape
    f = w1.shape[1]
    assert t % tm == 0 and f % tf == 0
    return pl.pallas_call(
        _ffn_kernel,
        grid=(t // tm, f // tf),
        in_specs=[
            pl.BlockSpec((tm, d), lambda i, j: (i, 0)),
            pl.BlockSpec((d, tf), lambda i, j: (0, j)),
            pl.BlockSpec((tf, d), lambda i, j: (j, 0)),
            _const_spec((1, d)), _const_spec((1, d)),
        ],
        out_specs=pl.BlockSpec((tm, d), lambda i, j: (i, 0)),
        out_shape=jax.ShapeDtypeStruct((t, d), F32),
        scratch_shapes=[pltpu.VMEM((tm, d), BF16)],
        compiler_params=pltpu.CompilerParams(
            dimension_semantics=("parallel", "arbitrary"), vmem_limit_bytes=VMEM_LIMIT_BIG),
        name="ffn",
    )(x1, w1, w2, g2.reshape(1, d), b2.reshape(1, d))


def kernel(x, positions, ln_in_g, ln_in_b, w_in, g_cq, w_uq, g_ckv, w_uk, w_uv, conv_w, conv_b,
           g_conv_ln, b_conv_ln, w_out, g_ln1, b_ln1, w_ff1, w_ff2, g_ln2, b_ln2):
    b, s, d = x.shape
    w_inT = w_in[0].T.astype(BF16)
    o_kr = Q_RANK + KV_RANK
    wc = w_inT[:o_kr]
    wkrT = w_inT[o_kr:o_kr + ROPE]
    wconv = w_inT[o_kr + ROPE:]
    wuqT = w_uq[0].T.astype(BF16)
    wuk = w_uk[0].astype(BF16)
    wuvT = w_uv[0].T.astype(BF16)

    cosT, sinT = _rope_tables(positions)
    qT, k, vT, u = _in_proj(x, ln_in_g, ln_in_b, wc, wkrT, wconv, g_cq[0], g_ckv[0],
                            wuqT, wuk, wuvT, cosT, sinT)
    y, w_out0 = _conv_module(u, conv_w[0], conv_b[0], w_out[0])
    attn = _attention(qT, k, vT)
    x2d = x.reshape(b * s, d)
    x1, w1, w2 = _merge(x2d, ln_in_g, ln_in_b, attn.reshape(b * s, MLA_WIDTH), y.reshape(b * s, SUB, LANE),
                        g_conv_ln[0], b_conv_ln[0], w_out0, g_ln1[0], b_ln1[0], w_ff1[0], w_ff2[0])
    out = _ffn(x1, w1, w2, g_ln2[0], b_ln2[0])
    return out.reshape(b, s, d)
```
